```python
import jax, jax.numpy as jnp
from jax import lax
import numpy as np

D_MODEL = 1024
BATCH = 8
SEQ = 4096
DEPTH = 1

CHUNK = 64
Q_BLOCK = 128
SB_WIDTH = D_MODEL // 2
SB_HEAD_DIM = 64
SB_HEADS = SB_WIDTH // SB_HEAD_DIM
ML_WIDTH = D_MODEL - SB_WIDTH
ML_HEADS = 4
ML_HEAD_DIM = ML_WIDTH // ML_HEADS
MIX_WIDTH = SB_WIDTH + ML_WIDTH
CONV_WIDTH = 4
IN_SIZES = (SB_WIDTH, SB_WIDTH, SB_WIDTH, 2 * ML_WIDTH, ML_WIDTH, ML_WIDTH, ML_HEADS, ML_HEADS)
IN_WIDTH = 3 * SB_WIDTH + 4 * ML_WIDTH + 2 * ML_HEADS
PEER_HEADS = 8
PEER_NKEYS = 128
PEER_EXPERTS = PEER_NKEYS * PEER_NKEYS
PEER_TOPK = 16
PEER_QDIM = 256
PEER_HALF = PEER_QDIM // 2
PEER_TOKEN_BLOCK = 128
N_MOD = 6
EPS = 1e-6

kernel_name = "hybrid_stickbreak_mlstm_peer_adaln"


def rms_norm(x, g):
    x32 = x.astype(jnp.float32)
    y = x32 * lax.rsqrt(jnp.mean(x32 * x32, axis=-1, keepdims=True) + EPS)
    return (y * g.astype(jnp.float32)).astype(x.dtype)


def causal_depthwise_conv(x, w):
    return lax.conv_general_dilated(
        x, w[:, None, :].astype(x.dtype), window_strides=(1,),
        padding=[(CONV_WIDTH - 1, 0)], dimension_numbers=('NWC', 'WIO', 'NWC'),
        feature_group_count=x.shape[-1])


def stick_breaking(q, k, v):
    s_len, d = q.shape[2], q.shape[-1]
    scale = d ** -0.5
    outs = []
    for t0 in range(0, s_len, Q_BLOCK):
        t1 = t0 + Q_BLOCK
        z = jnp.einsum('bhtd,bhsd->bhts', q[:, :, t0:t1], k[:, :, :t1]).astype(jnp.float32) * scale
        visible = jnp.arange(t1)[None, :] < jnp.arange(t0, t1)[:, None]
        log_keep = jnp.where(visible, jax.nn.log_sigmoid(-z), 0.0)
        log_between = lax.cumsum(log_keep, axis=3, reverse=True) - log_keep
        w = jnp.where(visible, jnp.exp(jax.nn.log_sigmoid(z) + log_between), 0.0)
        outs.append(jnp.einsum('bhts,bhsd->bhtd', w, v[:, :, :t1].astype(jnp.float32)))
    return jnp.concatenate(outs, axis=2).astype(v.dtype)


def mlstm(q, k, v, i_pre, f_pre):
    b, nh, s_len, d = q.shape
    nc = s_len // CHUNK
    f32 = jnp.float32

    def to_chunks(t):
        t = t.reshape((b, nh, nc, CHUNK) + t.shape[3:])
        return jnp.moveaxis(t, 2, 0)

    qc = to_chunks(q.astype(f32))
    kc = to_chunks(k.astype(f32) * (d ** -0.5))
    vc = to_chunks(v.astype(f32))
    a = jnp.cumsum(to_chunks(jax.nn.log_sigmoid(f_pre.astype(f32))), axis=-1)
    li = to_chunks(i_pre.astype(f32))
    causal = jnp.tril(jnp.ones((CHUNK, CHUNK), dtype=bool))

    def step(carry, inp):
        c_state, n_state, m_state = carry
        q_c, k_c, v_c, a_c, i_c = inp
        log_d = jnp.where(causal, a_c[..., :, None] - a_c[..., None, :] + i_c[..., None, :], -jnp.inf)
        log_inter = a_c + m_state[..., None]
        m_row = jnp.maximum(jnp.max(log_d, axis=-1), log_inter)
        w_intra = jnp.exp(log_d - m_row[..., None])
        w_inter = jnp.exp(log_inter - m_row)
        s_qk = jnp.einsum('bhtk,bhsk->bhts', q_c, k_c) * w_intra
        num = (jnp.einsum('bhts,bhsv->bhtv', s_qk, v_c)
               + w_inter[..., None] * jnp.einsum('bhvk,bhtk->bhtv', c_state, q_c))
        den = jnp.sum(s_qk, axis=-1) + w_inter * jnp.einsum('bhk,bhtk->bht', n_state, q_c)
        h = num / jnp.maximum(jnp.abs(den), jnp.exp(-m_row))[..., None]
        a_end = a_c[..., -1]
        log_w = a_end[..., None] - a_c + i_c
        m_new = jnp.maximum(a_end + m_state, jnp.max(log_w, axis=-1))
        w_s = jnp.exp(log_w - m_new[..., None])
        decay = jnp.exp(a_end + m_state - m_new)
        c_new = decay[..., None, None] * c_state + jnp.einsum('bhs,bhsv,bhsk->bhvk', w_s, v_c, k_c)
        n_new = decay[..., None] * n_state + jnp.einsum('bhs,bhsk->bhk', w_s, k_c)
        return (c_new, n_new, m_new), h

    init = (jnp.zeros((b, nh, d, d), f32), jnp.zeros((b, nh, d), f32), jnp.zeros((b, nh), f32))
    _, h = lax.scan(step, init, (qc, kc, vc, a, li))
    h = jnp.moveaxis(h, 0, 2).reshape(b, nh, s_len, d)
    return h.astype(v.dtype)


def peer(h, w_q, sub_keys, expert_u, expert_v):
    b, s_len, d = h.shape
    q = jnp.einsum('bsd,dq->bsq', h, w_q).reshape(b, s_len, PEER_HEADS, 2, PEER_HALF)
    scores = jnp.einsum('bshpc,hpnc->bshpn', q, sub_keys).astype(jnp.float32)
    top_s, top_i = lax.top_k(scores, PEER_TOPK)
    cand_s = (top_s[..., 0, :, None] + top_s[..., 1, None, :]).reshape(b, s_len, PEER_HEADS, PEER_TOPK * PEER_TOPK)
    cand_i = (top_i[..., 0, :, None] * PEER_NKEYS + top_i[..., 1, None, :]).reshape(b, s_len, PEER_HEADS, PEER_TOPK * PEER_TOPK)
    best_s, best_pos = lax.top_k(cand_s, PEER_TOPK)
    best_i = jnp.take_along_axis(cand_i, best_pos, axis=-1)
    gate = jax.nn.softmax(best_s, axis=-1)
    n_sel = PEER_HEADS * PEER_TOPK
    nb = (b * s_len) // PEER_TOKEN_BLOCK
    xs = h.reshape(nb, PEER_TOKEN_BLOCK, d)
    idx = best_i.reshape(nb, PEER_TOKEN_BLOCK, n_sel)
    gs = gate.reshape(nb, PEER_TOKEN_BLOCK, n_sel)

    def token_block(args):
        xb, ib, gb = args
        u = jnp.take(expert_u, ib, axis=0)
        act = jax.nn.gelu(jnp.einsum('td,tkd->tk', xb, u).astype(jnp.float32), approximate=False)
        vv = jnp.take(expert_v, ib, axis=0)
        return jnp.einsum('tk,tkd->td', (gb * act).astype(h.dtype), vv)

    y = lax.map(token_block, (xs, idx, gs))
    return y.reshape(b, s_len, d)


def setup_inputs(seed: int = 0) -> dict:
    key = jax.random.key(seed)
    ks = jax.random.split(key, 20)
    nrm = jax.random.normal
    D = D_MODEL
    return {
        "x": nrm(ks[0], (BATCH, SEQ, D), jnp.float32),
        "c": nrm(ks[1], (BATCH, D), jnp.float32),
        "w_ada": nrm(ks[2], (DEPTH, D, N_MOD * D), jnp.float32) * (0.5 * D ** -0.5),
        "b_ada": nrm(ks[3], (DEPTH, N_MOD * D), jnp.float32) * 0.02,
        "g_norm1": 1.0 + 0.02 * nrm(ks[4], (DEPTH, D), jnp.float32),
        "w_in": nrm(ks[5], (DEPTH, D, IN_WIDTH), jnp.float32) * D ** -0.5,
        "b_igate": 0.1 * nrm(ks[6], (DEPTH, ML_HEADS), jnp.float32),
        "b_fgate": 3.0 + 3.0 * jax.random.uniform(ks[7], (DEPTH, ML_HEADS), jnp.float32),
        "conv_w": nrm(ks[8], (DEPTH, CONV_WIDTH, 2 * ML_WIDTH), jnp.float32) * CONV_WIDTH ** -0.5,
        "g_q_sb": 1.0 + 0.02 * nrm(ks[9], (DEPTH, SB_HEAD_DIM), jnp.float32),
        "g_k_sb": 1.0 + 0.02 * nrm(ks[10], (DEPTH, SB_HEAD_DIM), jnp.float32),
        "g_out_sb": 1.0 + 0.02 * nrm(ks[11], (DEPTH, SB_HEADS, SB_HEAD_DIM), jnp.float32),
        "g_out_ml": 1.0 + 0.02 * nrm(ks[12], (DEPTH, ML_HEADS, ML_HEAD_DIM), jnp.float32),
        "w_out": nrm(ks[13], (DEPTH, MIX_WIDTH, D), jnp.float32) * MIX_WIDTH ** -0.5,
        "g_norm2": 1.0 + 0.02 * nrm(ks[14], (DEPTH, D), jnp.float32),
        "w_q_peer": nrm(ks[15], (DEPTH, D, PEER_HEADS * PEER_QDIM), jnp.float32) * D ** -0.5,
        "sub_keys": nrm(ks[16], (DEPTH, PEER_HEADS, 2, PEER_NKEYS, PEER_HALF), jnp.float32) * PEER_HALF ** -0.5,
        "expert_u": nrm(ks[17], (DEPTH, PEER_EXPERTS, D), jnp.float32) * D ** -0.5,
        "expert_v": nrm(ks[18], (DEPTH, PEER_EXPERTS, D), jnp.float32),
    }


def reference(x, c, w_ada, b_ada, g_norm1, w_in, b_igate, b_fgate, conv_w, g_q_sb, g_k_sb,
              g_out_sb, g_out_ml, w_out, g_norm2, w_q_peer, sub_keys, expert_u, expert_v):
    b, s_len, _ = x.shape
    cuts = []
    acc = 0
    for size in IN_SIZES[:-1]:
        acc += size
        cuts.append(acc)

    def to_heads(t, n):
        return t.reshape(b, s_len, n, -1).transpose(0, 2, 1, 3)

    for l in range(DEPTH):
        mod = jnp.einsum('bd,dm->bm', jax.nn.silu(c), w_ada[l]) + b_ada[l]
        shift1, scale1, gate1, shift2, scale2, gate2 = jnp.split(mod[:, None, :], N_MOD, axis=-1)

        h = rms_norm(x, g_norm1[l]) * (1 + scale1) + shift1
        proj = jnp.einsum('bsd,dn->bsn', h, w_in[l])
        q_sb, k_sb, v_sb, qk_ml, v_ml, o_ml, i_ml, f_ml = jnp.split(proj, cuts, axis=-1)

        q_sb = rms_norm(to_heads(q_sb, SB_HEADS), g_q_sb[l])
        k_sb = rms_norm(to_heads(k_sb, SB_HEADS), g_k_sb[l])
        y_sb = stick_breaking(q_sb, k_sb, to_heads(v_sb, SB_HEADS)).transpose(0, 2, 1, 3)
        y_sb = rms_norm(y_sb, g_out_sb[l])

        qk_ml = jax.nn.silu(causal_depthwise_conv(qk_ml, conv_w[l]))
        q_ml, k_ml = jnp.split(qk_ml, 2, axis=-1)
        i_pre = (i_ml + b_igate[l]).transpose(0, 2, 1)
        f_pre = (f_ml + b_fgate[l]).transpose(0, 2, 1)
        y_ml = mlstm(to_heads(q_ml, ML_HEADS), to_heads(k_ml, ML_HEADS), to_heads(v_ml, ML_HEADS), i_pre, f_pre)
        y_ml = rms_norm(y_ml.transpose(0, 2, 1, 3), g_out_ml[l]) * jax.nn.sigmoid(o_ml).reshape(b, s_len, ML_HEADS, ML_HEAD_DIM)

        mixed = jnp.concatenate([y_sb.reshape(b, s_len, SB_WIDTH), y_ml.reshape(b, s_len, ML_WIDTH)], axis=-1)
        x = x + gate1 * jnp.einsum('bsm,md->bsd', mixed, w_out[l])

        h2 = rms_norm(x, g_norm2[l]) * (1 + scale2) + shift2
        x = x + gate2 * peer(h2, w_q_peer[l], sub_keys[l], expert_u[l], expert_v[l])
    return x
```

```python
import functools
import math

import jax
import jax.numpy as jnp
from jax import lax
from jax.experimental import pallas as pl
from jax.experimental.pallas import tpu as pltpu

F32 = jnp.float32
BF16 = jnp.bfloat16

EPS = 1e-6
N_MOD = 6
MLSTM_CHUNK = 64
PEER_TOPK = 16
NEG_INF = float("-inf")

V7X_VMEM_BYTES = 64 * 1024 * 1024
VMEM_LIMIT = V7X_VMEM_BYTES - 12 * 1024 * 1024
LANES = 128

NT_DIMS = (((1,), (1,)), ((), ()))
NN_DIMS = (((1,), (0,)), ((), ()))
TN_DIMS = (((0,), (0,)), ((), ()))


def _params(*sem):
    return pltpu.CompilerParams(dimension_semantics=sem, vmem_limit_bytes=VMEM_LIMIT)


def _dot(a, b, dims=NN_DIMS):
    return lax.dot_general(a, b, dims, preferred_element_type=F32)


def _split_bf16(a):
    hi = a.astype(BF16)
    lo = (a - hi.astype(F32)).astype(BF16)
    return hi, lo


def _dot_f32(a, b, dims=NN_DIMS):
    ah, al = _split_bf16(a)
    bh, bl = _split_bf16(b)
    return _dot(ah, bh, dims) + (_dot(ah, bl, dims) + _dot(al, bh, dims))


def _dot_exact_rhs(a, b_bf16, dims=NN_DIMS):
    ah, al = _split_bf16(a)
    return _dot(ah, b_bf16, dims) + _dot(al, b_bf16, dims)


def _sigmoid(x):
    return 1.0 / (1.0 + jnp.exp(-x))


def _log_sigmoid_neg(z):
    return -(jnp.maximum(z, 0.0) + jnp.log(1.0 + jnp.exp(-jnp.abs(z))))


def _ada_kernel(c_ref, w_ref, b_ref, o_ref):
    c = c_ref[...]
    o_ref[...] = _dot_f32(c * _sigmoid(c), w_ref[...]) + b_ref[...]


def _ada(c, w, b):
    bsz, d = c.shape
    n = w.shape[1]
    tn = n // N_MOD
    return pl.pallas_call(
        _ada_kernel,
        grid=(n // tn,),
        in_specs=[pl.BlockSpec((bsz, d), lambda j: (0, 0)),
                  pl.BlockSpec((d, tn), lambda j: (0, j)),
                  pl.BlockSpec((1, tn), lambda j: (0, j))],
        out_specs=pl.BlockSpec((bsz, tn), lambda j: (0, j)),
        out_shape=jax.ShapeDtypeStruct((bsz, n), F32),
        compiler_params=_params("parallel"),
    )(c, w, b.reshape(1, n))


def _modulated_norm(x, g, scale, shift):
    y = x * lax.rsqrt(jnp.mean(x * x, axis=-1, keepdims=True) + EPS)
    return (y * g) * (1.0 + scale) + shift


def _inproj_kernel(x_ref, mod_ref, g_ref, w_ref, wg_ref, p_ref, gate_ref, *, n_chunk):
    h = _modulated_norm(x_ref[0], g_ref[...], mod_ref[0, 1:2, :], mod_ref[0, 0:1, :])
    hb = h.astype(BF16)
    for n0 in range(0, w_ref.shape[1], n_chunk):
        p_ref[0, :, n0:n0 + n_chunk] = _dot(hb, w_ref[:, n0:n0 + n_chunk]).astype(BF16)
    gate_ref[0] = _dot(hb, wg_ref[...])


def _inproj(x, mod, g, w_main, w_gate, tm):
    bsz, s, d = x.shape
    n = w_main.shape[1]
    ng = w_gate.shape[1]
    return pl.pallas_call(
        functools.partial(_inproj_kernel, n_chunk=512),
        grid=(bsz, s // tm),
        in_specs=[pl.BlockSpec((1, tm, d), lambda b, i: (b, i, 0)),
                  pl.BlockSpec((1, N_MOD, d), lambda b, i: (b, 0, 0)),
                  pl.BlockSpec((1, d), lambda b, i: (0, 0)),
                  pl.BlockSpec((d, n), lambda b, i: (0, 0)),
                  pl.BlockSpec((d, ng), lambda b, i: (0, 0))],
        out_specs=[pl.BlockSpec((1, tm, n), lambda b, i: (b, i, 0)),
                   pl.BlockSpec((1, tm, ng), lambda b, i: (b, i, 0))],
        out_shape=[jax.ShapeDtypeStruct((bsz, s, n), BF16),
                   jax.ShapeDtypeStruct((bsz, s, ng), F32)],
        compiler_params=_params("parallel", "parallel"),
    )(x, mod, g, w_main, w_gate)


def _pair_rms(x, g2, hd):
    lane = lax.broadcasted_iota(jnp.int32, x.shape, 1)
    sq = x * x
    s0 = jnp.sum(jnp.where(lane < hd, sq, 0.0), axis=-1, keepdims=True)
    s1 = jnp.sum(jnp.where(lane < hd, 0.0, sq), axis=-1, keepdims=True)
    inv = jnp.where(lane < hd, lax.rsqrt(s0 / hd + EPS), lax.rsqrt(s1 / hd + EPS))
    return x * inv * g2


def _sb_kernel(q_ref, k_ref, v_ref, gq_ref, gk_ref, go_ref, o_ref, kn_ref, *, tq, hd):
    qi = pl.program_id(2)
    s = k_ref.shape[1]
    rows = 512 if s % 512 == 0 else tq

    @pl.when(qi == 0)
    def _():
        for r0 in range(0, s, rows):
            kk = k_ref[0, r0:r0 + rows, :].astype(F32)
            kn_ref[r0:r0 + rows, :] = _pair_rms(kk, gk_ref[...], hd).astype(BF16)

    qn = _pair_rms(q_ref[0].astype(F32), gq_ref[...], hd) * (hd ** -0.5)
    lane = lax.broadcasted_iota(jnp.int32, qn.shape, 1)
    q_heads = (jnp.where(lane < hd, qn, 0.0).astype(BF16),
               jnp.where(lane < hd, 0.0, qn).astype(BF16))

    r = lax.broadcasted_iota(jnp.int32, (tq, tq), 0)
    c = lax.broadcasted_iota(jnp.int32, (tq, tq), 1)
    suffix = jnp.where(r >= c, 1.0, 0.0).astype(BF16)
    visible = c < r

    def block(j, carry, diagonal):
        k0 = pl.multiple_of(j * tq, tq)
        kb = kn_ref[pl.ds(k0, tq), :]
        vb = v_ref[0, pl.ds(k0, tq), :]
        out = []
        for h in range(2):
            acc, run = carry[h]
            z = _dot(q_heads[h], kb, NT_DIMS)
            lk = _log_sigmoid_neg(z)
            if diagonal:
                lk = jnp.where(visible, lk, 0.0)
            cs = _dot_exact_rhs(lk, suffix)
            w = jnp.exp(z + cs + run)
            if diagonal:
                w = jnp.where(visible, w, 0.0)
            acc = acc + _dot(w.astype(BF16), vb)
            out.append((acc, run + cs[:, 0:1]))
        return tuple(out)

    zero = (jnp.zeros((tq, 2 * hd), F32), jnp.zeros((tq, 1), F32))
    carry = block(qi, (zero, zero), True)
    carry = lax.fori_loop(0, qi, lambda it, cr: block(qi - 1 - it, cr, False), carry)

    y = jnp.where(lane < hd, carry[0][0], carry[1][0])
    o_ref[0] = _pair_rms(y, go_ref[0], hd).astype(o_ref.dtype)


def _sb_attention(p, g_q2, g_k2, g_out2, n_heads, hd, tq):
    bsz, s, _ = p.shape
    w = 2 * hd
    npair = n_heads // 2
    return pl.pallas_call(
        functools.partial(_sb_kernel, tq=tq, hd=hd),
        grid=(bsz, npair, s // tq),
        in_specs=[pl.BlockSpec((1, tq, w), lambda b, h, i: (b, i, h)),
                  pl.BlockSpec((1, s, w), lambda b, h, i: (b, 0, npair + h)),
                  pl.BlockSpec((1, s, w), lambda b, h, i: (b, 0, 2 * npair + h)),
                  pl.BlockSpec((1, w), lambda b, h, i: (0, 0)),
                  pl.BlockSpec((1, w), lambda b, h, i: (0, 0)),
                  pl.BlockSpec((1, 1, w), lambda b, h, i: (h, 0, 0))],
        out_specs=pl.BlockSpec((1, tq, w), lambda b, h, i: (b, i, h)),
        out_shape=jax.ShapeDtypeStruct((bsz, s, n_heads * hd), BF16),
        scratch_shapes=[pltpu.VMEM((s, w), BF16)],
        compiler_params=_params("parallel", "parallel", "arbitrary"),
    )(p, p, p, g_q2, g_k2, g_out2)


def _ml_kernel(q_ref, k_ref, v_ref, o_ref, gate_ref, cwq_ref, cwk_ref, big_ref, bfg_ref,
               gout_ref, y_ref, qpad_ref, kpad_ref, *, heads_per_step, dh, conv_width):
    L = MLSTM_CHUNK
    s = q_ref.shape[1]
    n_heads = gate_ref.shape[2] // 2
    hp = pl.program_id(1)
    halo = 8
    rows = 512 if s % 512 == 0 else L

    for src, dst in ((q_ref, qpad_ref), (k_ref, kpad_ref)):
        dst[0:halo, :] = jnp.zeros((halo, dst.shape[1]), F32)
        for r0 in range(0, s, rows):
            dst[halo + r0:halo + r0 + rows, :] = src[0, r0:r0 + rows, :].astype(F32)

    r = lax.broadcasted_iota(jnp.int32, (L, L), 0)
    c = lax.broadcasted_iota(jnp.int32, (L, L), 1)
    eye = r == c
    causal = c <= r
    prefix = jnp.where(causal, 1.0, 0.0).astype(BF16)
    glane = lax.broadcasted_iota(jnp.int32, (1, 2 * n_heads), 1)

    def conv_silu(pad_ref, w_ref, t0):
        ext = pad_ref[pl.ds(t0, L + halo), :]
        y = jnp.zeros((L, ext.shape[1]), F32)
        for j in range(conv_width):
            off = halo - (conv_width - 1) + j
            y = y + ext[off:off + L, :] * w_ref[j:j + 1, :]
        return y * _sigmoid(y)

    def to_row(col):
        return jnp.sum(jnp.where(eye, col, 0.0), axis=0, keepdims=True)

    def chunk(ci, carry):
        t0 = pl.multiple_of(ci * L, L)
        g = gate_ref[0, pl.ds(t0, L), :]
        i_all = g + big_ref[...]
        lf_hi, lf_lo = _split_bf16(_log_sigmoid_neg(-(g + bfg_ref[...])))
        a_all = _dot(prefix, lf_hi) + _dot(prefix, lf_lo)
        qc = conv_silu(qpad_ref, cwq_ref, t0)
        kc = conv_silu(kpad_ref, cwk_ref, t0) * (dh ** -0.5)
        new_carry = []
        for hh in range(heads_per_step):
            ct, n_row, m_st = carry[hh]
            col = hp * heads_per_step + hh
            pick_i = glane == col
            pick_f = glane == (n_heads + col)
            i_col = jnp.sum(jnp.where(pick_i, i_all, 0.0), axis=1, keepdims=True)
            a_col = jnp.sum(jnp.where(pick_f, a_all, 0.0), axis=1, keepdims=True)
            a_row = to_row(a_col)
            i_row = to_row(i_col)
            sl = slice(hh * dh, (hh + 1) * dh)
            q_h = qc[:, sl].astype(BF16)
            k_h = kc[:, sl]
            k_hb = k_h.astype(BF16)
            v_h = v_ref[0, pl.ds(t0, L), sl]

            log_d = jnp.where(causal, a_col - a_row + i_row, NEG_INF)
            log_inter = a_col + m_st
            m_row = jnp.maximum(jnp.max(log_d, axis=-1, keepdims=True), log_inter)
            w_intra = jnp.exp(log_d - m_row)
            w_inter = jnp.exp(log_inter - m_row)
            s_qk = _dot(q_h, k_hb, NT_DIMS) * w_intra
            num = _dot(s_qk.astype(BF16), v_h) + w_inter * _dot(q_h, ct.astype(BF16))
            qn = jnp.sum(q_h.astype(F32) * n_row, axis=-1, keepdims=True)
            den = jnp.sum(s_qk, axis=-1, keepdims=True) + w_inter * qn
            h = num / jnp.maximum(jnp.abs(den), jnp.exp(-m_row))

            a_end = a_col[L - 1:L, :]
            log_w = a_end - a_col + i_col
            m_new = jnp.maximum(a_end + m_st, jnp.max(log_w, axis=0, keepdims=True))
            w_s = jnp.exp(log_w - m_new)
            decay = jnp.exp(a_end + m_st - m_new)
            kw = k_h * w_s
            ct_new = decay * ct + _dot(kw.astype(BF16), v_h, TN_DIMS)
            n_new = decay * n_row + jnp.sum(kw, axis=0, keepdims=True)
            new_carry.append((ct_new, n_new, m_new))

            ms = jnp.mean(h * h, axis=-1, keepdims=True)
            og = _sigmoid(o_ref[0, pl.ds(t0, L), sl].astype(F32))
            y = h * lax.rsqrt(ms + EPS) * gout_ref[0, hh:hh + 1, :] * og
            y_ref[0, pl.ds(t0, L), sl] = y.astype(y_ref.dtype)
        return tuple(new_carry)

    init = tuple((jnp.zeros((dh, dh), F32), jnp.zeros((1, dh), F32), jnp.zeros((1, 1), F32))
                 for _ in range(heads_per_step))
    lax.fori_loop(0, s // L, chunk, init)


def _mlstm(p, gates, conv_w, b_i, b_f, g_out, col0, n_heads, dh, heads_per_step):
    bsz, s, _ = p.shape
    w = heads_per_step * dh
    nstep = n_heads // heads_per_step
    width = n_heads * dh
    cb = col0 // w
    wb = width // w
    conv_width = conv_w.shape[0]
    bias = jnp.concatenate([b_i, b_f]).reshape(1, 2 * n_heads)
    lane = jnp.arange(2 * n_heads) < n_heads
    big = jnp.where(lane, bias, 0.0)
    bfg = jnp.where(lane, 0.0, bias)
    kern = functools.partial(_ml_kernel, heads_per_step=heads_per_step, dh=dh, conv_width=conv_width)
    seq = lambda off: pl.BlockSpec((1, s, w), lambda b, h: (b, 0, cb + off + h))
    return pl.pallas_call(
        kern,
        grid=(bsz, nstep),
        in_specs=[seq(0), seq(wb), seq(2 * wb), seq(3 * wb),
                  pl.BlockSpec((1, s, 2 * n_heads), lambda b, h: (b, 0, 0)),
                  pl.BlockSpec((conv_width, w), lambda b, h: (0, h)),
                  pl.BlockSpec((conv_width, w), lambda b, h: (0, wb + h)),
                  pl.BlockSpec((1, 2 * n_heads), lambda b, h: (0, 0)),
                  pl.BlockSpec((1, 2 * n_heads), lambda b, h: (0, 0)),
                  pl.BlockSpec((1, heads_per_step, dh), lambda b, h: (h, 0, 0))],
        out_specs=pl.BlockSpec((1, s, w), lambda b, h: (b, 0, h)),
        out_shape=jax.ShapeDtypeStruct((bsz, s, width), BF16),
        scratch_shapes=[pltpu.VMEM((s + 8, w), F32), pltpu.VMEM((s + 8, w), F32)],
        compiler_params=_params("parallel", "parallel"),
    )(p, p, p, p, gates, conv_w, conv_w, big, bfg,
      g_out.reshape(nstep, heads_per_step, dh))


def _outproj_kernel(ysb_ref, yml_ref, x_ref, mod_ref, g_ref, wo_ref, wq_ref, keys_ref,
                    x1_ref, h2_ref, sc_ref):
    wsb = ysb_ref.shape[2]
    mix = _dot(ysb_ref[0], wo_ref[0:wsb, :]) + _dot(yml_ref[0], wo_ref[wsb:, :])
    x1 = x_ref[0] + mod_ref[0, 2:3, :] * mix
    x1_ref[0] = x1
    h2 = _modulated_norm(x1, g_ref[...], mod_ref[0, 4:5, :], mod_ref[0, 3:4, :]).astype(BF16)
    h2_ref[0] = h2
    half = keys_ref.shape[2]
    for i in range(keys_ref.shape[0]):
        qb = _dot(h2, wq_ref[:, i * half:(i + 1) * half]).astype(BF16)
        sc_ref[i, 0] = _dot(keys_ref[i], qb, NT_DIMS)


def _outproj(ysb, yml, x, mod, g, w_out, w_q, keys, tm):
    bsz, s, d = x.shape
    nsk, nk, half = keys.shape
    return pl.pallas_call(
        _outproj_kernel,
        grid=(bsz, s // tm),
        in_specs=[pl.BlockSpec((1, tm, ysb.shape[2]), lambda b, i: (b, i, 0)),
                  pl.BlockSpec((1, tm, yml.shape[2]), lambda b, i: (b, i, 0)),
                  pl.BlockSpec((1, tm, d), lambda b, i: (b, i, 0)),
                  pl.BlockSpec((1, N_MOD, d), lambda b, i: (b, 0, 0)),
                  pl.BlockSpec((1, d), lambda b, i: (0, 0)),
                  pl.BlockSpec(w_out.shape, lambda b, i: (0, 0)),
                  pl.BlockSpec(w_q.shape, lambda b, i: (0, 0)),
                  pl.BlockSpec(keys.shape, lambda b, i: (0, 0, 0))],
        out_specs=[pl.BlockSpec((1, tm, d), lambda b, i: (b, i, 0)),
                   pl.BlockSpec((1, tm, d), lambda b, i: (b, i, 0)),
                   pl.BlockSpec((nsk, 1, nk, tm), lambda b, i: (0, b, 0, i))],
        out_shape=[jax.ShapeDtypeStruct((bsz, s, d), F32),
                   jax.ShapeDtypeStruct((bsz, s, d), BF16),
                   jax.ShapeDtypeStruct((nsk, bsz, nk, s), F32)],
        compiler_params=_params("parallel", "parallel"),
    )(ysb, yml, x, mod, g, w_out, w_q, keys)


def _extract_top(x, k, pos, on_pick):
    n = x.shape[0]
    for a in range(k):
        m = jnp.max(x, axis=0, keepdims=True)
        first = jnp.min(jnp.where(x == m, pos, n), axis=0, keepdims=True)
        pick = pos == first
        on_pick(a, m, pick)
        x = jnp.where(pick, NEG_INF, x)


def _route_kernel(sc_ref, cnt_ref, c0_ref, rank_ref, e1_ref, top_ref, cand_ref, sel_ref):
    n_heads = cnt_ref.shape[0]
    nk, tb = sc_ref.shape[2], sc_ref.shape[3]
    K = PEER_TOPK
    pos = lax.broadcasted_iota(jnp.int32, (nk, tb), 0)
    cpos = lax.broadcasted_iota(jnp.int32, (K * K, tb), 0)

    for h in range(n_heads):
        ranks = []
        for p in range(2):
            state = {"rank": jnp.full((nk, tb), float(K), F32)}

            def on_pick(a, m, pick, p=p, state=state):
                top_ref[p, a:a + 1, :] = m
                state["rank"] = jnp.where(pick, float(a), state["rank"])

            _extract_top(sc_ref[2 * h + p, 0], K, pos, on_pick)
            ranks.append(state["rank"])

        v1 = top_ref[1]
        for a in range(K):
            cand_ref[a * K:(a + 1) * K, :] = top_ref[0, a:a + 1, :] + v1
        cand = cand_ref[...]
        state = {"sel": jnp.zeros((K * K, tb), F32)}

        def on_pick2(a, m, pick, state=state):
            state["sel"] = jnp.where(pick, 1.0, state["sel"])

        _extract_top(cand, K, cpos, on_pick2)
        sel = state["sel"]
        m_top = top_ref[0, 0:1, :] + top_ref[1, 0:1, :]
        z = jnp.sum(sel * jnp.exp(cand - m_top), axis=0, keepdims=True)
        sel_ref[...] = sel
        cnt = jnp.zeros((nk, tb), F32)
        for a in range(K):
            nb = jnp.sum(sel_ref[a * K:(a + 1) * K, :], axis=0, keepdims=True)
            cnt = jnp.where(ranks[0] == float(a), nb, cnt)
        cnt_ref[h, 0] = cnt
        c0_ref[h, 0] = jnp.exp(sc_ref[2 * h, 0] - top_ref[0, 0:1, :]) / z
        rank_ref[h, 0] = ranks[1]
        e1_ref[h, 0] = jnp.exp(sc_ref[2 * h + 1, 0] - top_ref[1, 0:1, :])


def _route(scores, tb):
    nsk, bsz, nk, s = scores.shape
    n_heads = nsk // 2
    out = jax.ShapeDtypeStruct((n_heads, bsz, nk, s), F32)
    spec = pl.BlockSpec((n_heads, 1, nk, tb), lambda b, i: (0, b, 0, i))
    return pl.pallas_call(
        _route_kernel,
        grid=(bsz, s // tb),
        in_specs=[pl.BlockSpec((nsk, 1, nk, tb), lambda b, i: (0, b, 0, i))],
        out_specs=[spec] * 4,
        out_shape=[out] * 4,
        scratch_shapes=[pltpu.VMEM((2, PEER_TOPK, tb), F32),
                        pltpu.VMEM((PEER_TOPK * PEER_TOPK, tb), F32),
                        pltpu.VMEM((PEER_TOPK * PEER_TOPK, tb), F32)],
        compiler_params=_params("parallel", "parallel"),
    )(scores)


def _gelu(x):
    return 0.5 * x * (1.0 + lax.erf(x * math.sqrt(0.5)))


def _peer_kernel(h2_ref, u_ref, vt_ref, cnt_ref, c0_ref, rank_ref, e1_ref, x1_ref, mod_ref,
                 o_ref, acc_ref, coef_ref, *, groups):
    j = pl.program_id(2)
    n_heads, _, nk, tb = cnt_ref.shape

    @pl.when(j == 0)
    def _():
        acc_ref[...] = jnp.zeros_like(acc_ref)

    act = _dot(u_ref[...], h2_ref[0], NT_DIMS)
    for gg in range(groups):
        g = j * groups + gg
        cf = jnp.zeros((nk, tb), F32)
        for h in range(n_heads):
            n0 = cnt_ref[h, 0, pl.ds(g, 1), :]
            c0 = c0_ref[h, 0, pl.ds(g, 1), :]
            cf = cf + jnp.where(rank_ref[h, 0] < n0, e1_ref[h, 0] * c0, 0.0)
        coef_ref[gg * nk:(gg + 1) * nk, :] = (cf * _gelu(act[gg * nk:(gg + 1) * nk, :])).astype(BF16)
    acc_ref[...] += _dot(vt_ref[...], coef_ref[...])

    @pl.when(j == pl.num_programs(2) - 1)
    def _():
        o_ref[0] = x1_ref[0] + mod_ref[0, 5:6, :] * acc_ref[...].T


def _peer(h2, u, vt, cnt, c0, rank, e1, x1, mod, tb, groups):
    bsz, s, d = h2.shape
    n_heads, _, nk, _ = cnt.shape
    te = groups * nk
    n_exp = u.shape[0]
    rspec = pl.BlockSpec((n_heads, 1, nk, tb), lambda b, i, j: (0, b, 0, i))
    return pl.pallas_call(
        functools.partial(_peer_kernel, groups=groups),
        grid=(bsz, s // tb, n_exp // te),
        in_specs=[pl.BlockSpec((1, tb, d), lambda b, i, j: (b, i, 0)),
                  pl.BlockSpec((te, d), lambda b, i, j: (j, 0)),
                  pl.BlockSpec((d, te), lambda b, i, j: (0, j)),
                  rspec, rspec, rspec, rspec,
                  pl.BlockSpec((1, tb, d), lambda b, i, j: (b, i, 0)),
                  pl.BlockSpec((1, N_MOD, d), lambda b, i, j: (b, 0, 0))],
        out_specs=pl.BlockSpec((1, tb, d), lambda b, i, j: (b, i, 0)),
        out_shape=jax.ShapeDtypeStruct((bsz, s, d), F32),
        scratch_shapes=[pltpu.VMEM((d, tb), F32), pltpu.VMEM((te, tb), BF16)],
        compiler_params=_params("parallel", "parallel", "arbitrary"),
    )(h2, u, vt, cnt, c0, rank, e1, x1, mod)


def _layer(x, c, w_ada, b_ada, g_norm1, w_in, b_igate, b_fgate, conv_w, g_q_sb, g_k_sb,
           g_out_sb, g_out_ml, w_out, g_norm2, w_q_peer, sub_keys, expert_u, expert_v):
    bsz, s, d = x.shape
    sb_heads, sb_hd = g_out_sb.shape
    ml_heads, ml_hd = g_out_ml.shape
    sb_width = sb_heads * sb_hd
    ml_width = ml_heads * ml_hd
    n_main = 3 * sb_width + 4 * ml_width
    peer_heads, _, n_keys, half = sub_keys.shape

    tm = min(512, s)
    tq = min(128, s)
    tb = min(512, s)

    mod = _ada(c, w_ada, b_ada).reshape(bsz, N_MOD, d)
    p, gates = _inproj(x, mod, g_norm1.reshape(1, d), w_in[:, :n_main].astype(BF16),
                       w_in[:, n_main:].astype(BF16), tm)

    tile2 = lambda g: jnp.concatenate([g, g], axis=-1)
    ysb = _sb_attention(p, tile2(g_q_sb).reshape(1, 2 * sb_hd), tile2(g_k_sb).reshape(1, 2 * sb_hd),
                        g_out_sb.reshape(sb_heads // 2, 1, 2 * sb_hd), sb_heads, sb_hd, tq)
    yml = _mlstm(p, gates, conv_w, b_igate, b_fgate, g_out_ml, 3 * sb_width, ml_heads, ml_hd,
                 heads_per_step=2)

    x1, h2, scores = _outproj(ysb, yml, x, mod, g_norm2.reshape(1, d), w_out.astype(BF16),
                              w_q_peer.astype(BF16),
                              sub_keys.reshape(2 * peer_heads, n_keys, half).astype(BF16),
                              min(256, s))
    cnt, c0, rank, e1 = _route(scores, min(256, s))
    return _peer(h2, expert_u.astype(BF16), expert_v.T.astype(BF16), cnt, c0, rank, e1, x1, mod,
                 tb, groups=4)


def kernel(x, c, w_ada, b_ada, g_norm1, w_in, b_igate, b_fgate, conv_w, g_q_sb, g_k_sb, g_out_sb,
           g_out_ml, w_out, g_norm2, w_q_peer, sub_keys, expert_u, expert_v):
    params = (w_ada, b_ada, g_norm1, w_in, b_igate, b_fgate, conv_w, g_q_sb, g_k_sb, g_out_sb,
              g_out_ml, w_out, g_norm2, w_q_peer, sub_keys, expert_u, expert_v)
    for layer in range(w_ada.shape[0]):
        x = _layer(x, c, *(t[layer] for t in params))
    return x
```

```python
import functools
import math

import jax
import jax.numpy as jnp
from jax import lax
from jax.experimental import pallas as pl
from jax.experimental.pallas import tpu as pltpu

F32 = jnp.float32
BF16 = jnp.bfloat16

EPS = 1e-6
N_MOD = 6
MLSTM_CHUNK = 64
PEER_TOPK = 16
NEG_INF = float("-inf")
F32_EXP_UNDERFLOW = math.log(2.0 ** -126)

V7X_VMEM_BYTES = 64 * 1024 * 1024
VMEM_LIMIT = V7X_VMEM_BYTES - 12 * 1024 * 1024
LANES = 128
BF16_SUBLANES = 16

NT_DIMS = (((1,), (1,)), ((), ()))
NN_DIMS = (((1,), (0,)), ((), ()))
TN_DIMS = (((0,), (0,)), ((), ()))


def _params(*sem):
    return pltpu.CompilerParams(dimension_semantics=sem, vmem_limit_bytes=VMEM_LIMIT)


def _dot(a, b, dims=NN_DIMS):
    return lax.dot_general(a, b, dims, preferred_element_type=F32)


def _split_bf16(a):
    hi = a.astype(BF16)
    lo = (a - hi.astype(F32)).astype(BF16)
    return hi, lo


def _dot_f32(a, b, dims=NN_DIMS):
    ah, al = _split_bf16(a)
    bh, bl = _split_bf16(b)
    return _dot(ah, bh, dims) + (_dot(ah, bl, dims) + _dot(al, bh, dims))


def _dot_exact_rhs(a, b_bf16, dims=NN_DIMS):
    ah, al = _split_bf16(a)
    return _dot(ah, b_bf16, dims) + _dot(al, b_bf16, dims)


def _sigmoid(x):
    return 1.0 / (1.0 + jnp.exp(-x))


def _log_sigmoid_neg(z):
    return -(jnp.maximum(z, 0.0) + jnp.log(1.0 + jnp.exp(-jnp.abs(z))))


def _ada_kernel(c_ref, w_ref, b_ref, o_ref):
    c = c_ref[...]
    o_ref[...] = _dot_f32(c * _sigmoid(c), w_ref[...]) + b_ref[...]


def _ada(c, w, b):
    bsz, d = c.shape
    n = w.shape[1]
    tn = n // N_MOD
    return pl.pallas_call(
        _ada_kernel,
        grid=(n // tn,),
        in_specs=[pl.BlockSpec((bsz, d), lambda j: (0, 0)),
                  pl.BlockSpec((d, tn), lambda j: (0, j)),
                  pl.BlockSpec((1, tn), lambda j: (0, j))],
        out_specs=pl.BlockSpec((bsz, tn), lambda j: (0, j)),
        out_shape=jax.ShapeDtypeStruct((bsz, n), F32),
        compiler_params=_params("parallel"),
    )(c, w, b.reshape(1, n))


def _modulated_norm(x, g, scale, shift):
    y = x * lax.rsqrt(jnp.mean(x * x, axis=-1, keepdims=True) + EPS)
    return (y * g) * (1.0 + scale) + shift


def _inproj_kernel(x_ref, mod_ref, g_ref, w_ref, wg_ref, p_ref, gate_ref, *, n_chunk):
    h = _modulated_norm(x_ref[0], g_ref[...], mod_ref[0, 1:2, :], mod_ref[0, 0:1, :])
    hb = h.astype(BF16)
    for n0 in range(0, w_ref.shape[1], n_chunk):
        p_ref[0, :, n0:n0 + n_chunk] = _dot(hb, w_ref[:, n0:n0 + n_chunk]).astype(BF16)
    gate_ref[0] = _dot(hb, wg_ref[...])


def _inproj(x, mod, g, w_main, w_gate, tm):
    bsz, s, d = x.shape
    n = w_main.shape[1]
    ng = w_gate.shape[1]
    return pl.pallas_call(
        functools.partial(_inproj_kernel, n_chunk=512),
        grid=(bsz, s // tm),
        in_specs=[pl.BlockSpec((1, tm, d), lambda b, i: (b, i, 0)),
                  pl.BlockSpec((1, N_MOD, d), lambda b, i: (b, 0, 0)),
                  pl.BlockSpec((1, d), lambda b, i: (0, 0)),
                  pl.BlockSpec((d, n), lambda b, i: (0, 0)),
                  pl.BlockSpec((d, ng), lambda b, i: (0, 0))],
        out_specs=[pl.BlockSpec((1, tm, n), lambda b, i: (b, i, 0)),
                   pl.BlockSpec((1, tm, ng), lambda b, i: (b, i, 0))],
        out_shape=[jax.ShapeDtypeStruct((bsz, s, n), BF16),
                   jax.ShapeDtypeStruct((bsz, s, ng), F32)],
        compiler_params=_params("parallel", "parallel"),
    )(x, mod, g, w_main, w_gate)


def _pair_rms(x, g2, hd):
    lane = lax.broadcasted_iota(jnp.int32, x.shape, 1)
    sq = x * x
    s0 = jnp.sum(jnp.where(lane < hd, sq, 0.0), axis=-1, keepdims=True)
    s1 = jnp.sum(jnp.where(lane < hd, 0.0, sq), axis=-1, keepdims=True)
    inv = jnp.where(lane < hd, lax.rsqrt(s0 / hd + EPS), lax.rsqrt(s1 / hd + EPS))
    return x * inv * g2


def _sb_kernel(q_ref, k_ref, v_ref, gq_ref, gk_ref, go_ref, o_ref, kn_ref, acc_ref, run_ref,
               *, tq, hd):
    qi = pl.program_id(2)
    s = k_ref.shape[1]
    rows = 512 if s % 512 == 0 else tq

    @pl.when(qi == 0)
    def _():
        for r0 in range(0, s, rows):
            kk = k_ref[0, r0:r0 + rows, :].astype(F32)
            kn_ref[r0:r0 + rows, :] = _pair_rms(kk, gk_ref[...], hd).astype(BF16)

    qn = _pair_rms(q_ref[0].astype(F32), gq_ref[...], hd) * (hd ** -0.5)
    lane = lax.broadcasted_iota(jnp.int32, qn.shape, 1)
    q_heads = (jnp.where(lane < hd, qn, 0.0).astype(BF16),
               jnp.where(lane < hd, 0.0, qn).astype(BF16))

    r = lax.broadcasted_iota(jnp.int32, (tq, tq), 0)
    c = lax.broadcasted_iota(jnp.int32, (tq, tq), 1)
    suffix = jnp.where(r >= c, 1.0, 0.0).astype(BF16)
    visible = c < r

    def block(j, diagonal):
        k0 = pl.multiple_of(j * tq, tq)
        kb = kn_ref[pl.ds(k0, tq), :]
        vb = v_ref[0, pl.ds(k0, tq), :]
        top = []
        for h in range(2):
            z = _dot(q_heads[h], kb, NT_DIMS)
            lk = _log_sigmoid_neg(z)
            if diagonal:
                lk = jnp.where(visible, lk, 0.0)
            cs = _dot_exact_rhs(lk, suffix)
            run = run_ref[h]
            w = jnp.exp(z + cs + run)
            if diagonal:
                w = jnp.where(visible, w, 0.0)
            acc_ref[h] += _dot(w.astype(BF16), vb)
            run = run + cs[:, 0:1]
            run_ref[h] = run
            top.append(jnp.max(run))
        return jnp.maximum(top[0], top[1])

    acc_ref[...] = jnp.zeros_like(acc_ref)
    run_ref[...] = jnp.zeros_like(run_ref)
    z_bound = 1.02 * (hd ** 0.5) * jnp.max(jnp.abs(gq_ref[...])) * jnp.max(jnp.abs(gk_ref[...]))

    def more(state):
        j, top = state
        return jnp.logical_and(j >= 0, top + z_bound > F32_EXP_UNDERFLOW)

    lax.while_loop(more, lambda st: (st[0] - 1, block(st[0], False)), (qi - 1, block(qi, True)))

    y = jnp.where(lane < hd, acc_ref[0], acc_ref[1])
    o_ref[0] = _pair_rms(y, go_ref[0], hd).astype(o_ref.dtype)


def _sb_attention(p, g_q2, g_k2, g_out2, n_heads, hd, tq):
    bsz, s, _ = p.shape
    w = 2 * hd
    npair = n_heads // 2
    return pl.pallas_call(
        functools.partial(_sb_kernel, tq=tq, hd=hd),
        grid=(bsz, npair, s // tq),
        in_specs=[pl.BlockSpec((1, tq, w), lambda b, h, i: (b, i, h)),
                  pl.BlockSpec((1, s, w), lambda b, h, i: (b, 0, npair + h)),
                  pl.BlockSpec((1, s, w), lambda b, h, i: (b, 0, 2 * npair + h)),
                  pl.BlockSpec((1, w), lambda b, h, i: (0, 0)),
                  pl.BlockSpec((1, w), lambda b, h, i: (0, 0)),
                  pl.BlockSpec((1, 1, w), lambda b, h, i: (h, 0, 0))],
        out_specs=pl.BlockSpec((1, tq, w), lambda b, h, i: (b, i, h)),
        out_shape=jax.ShapeDtypeStruct((bsz, s, n_heads * hd), BF16),
        scratch_shapes=[pltpu.VMEM((s, w), BF16), pltpu.VMEM((2, tq, w), F32),
                        pltpu.VMEM((2, tq, 1), F32)],
        compiler_params=_params("parallel", "parallel", "arbitrary"),
    )(p, p, p, g_q2, g_k2, g_out2)


def _ml_kernel(q_ref, k_ref, v_ref, o_ref, gate_ref, cwq_ref, cwk_ref, big_ref, bfg_ref,
               gout_ref, y_ref, qpad_ref, kpad_ref, *, heads_per_step, dh, conv_width):
    L = MLSTM_CHUNK
    s = q_ref.shape[1]
    n_heads = gate_ref.shape[2] // 2
    hp = pl.program_id(1)
    halo = 8
    rows = 512 if s % 512 == 0 else L

    for src, dst in ((q_ref, qpad_ref), (k_ref, kpad_ref)):
        dst[0:halo, :] = jnp.zeros((halo, dst.shape[1]), F32)
        for r0 in range(0, s, rows):
            dst[halo + r0:halo + r0 + rows, :] = src[0, r0:r0 + rows, :].astype(F32)

    r = lax.broadcasted_iota(jnp.int32, (L, L), 0)
    c = lax.broadcasted_iota(jnp.int32, (L, L), 1)
    eye = r == c
    causal = c <= r
    prefix = jnp.where(causal, 1.0, 0.0).astype(BF16)
    glane = lax.broadcasted_iota(jnp.int32, (1, 2 * n_heads), 1)

    def conv_silu(pad_ref, w_ref, t0):
        ext = pad_ref[pl.ds(t0, L + halo), :]
        y = jnp.zeros((L, ext.shape[1]), F32)
        for j in range(conv_width):
            off = halo - (conv_width - 1) + j
            y = y + ext[off:off + L, :] * w_ref[j:j + 1, :]
        return y * _sigmoid(y)

    def to_row(col):
        return jnp.sum(jnp.where(eye, col, 0.0), axis=0, keepdims=True)

    def chunk(ci, carry):
        t0 = pl.multiple_of(ci * L, L)
        g = gate_ref[0, pl.ds(t0, L), :]
        i_all = g + big_ref[...]
        lf_hi, lf_lo = _split_bf16(_log_sigmoid_neg(-(g + bfg_ref[...])))
        a_all = _dot(prefix, lf_hi) + _dot(prefix, lf_lo)
        qc = conv_silu(qpad_ref, cwq_ref, t0)
        kc = conv_silu(kpad_ref, cwk_ref, t0) * (dh ** -0.5)
        new_carry = []
        for hh in range(heads_per_step):
            ct, n_row, m_st = carry[hh]
            col = hp * heads_per_step + hh
            pick_i = glane == col
            pick_f = glane == (n_heads + col)
            i_col = jnp.sum(jnp.where(pick_i, i_all, 0.0), axis=1, keepdims=True)
            a_col = jnp.sum(jnp.where(pick_f, a_all, 0.0), axis=1, keepdims=True)
            a_row = to_row(a_col)
            i_row = to_row(i_col)
            sl = slice(hh * dh, (hh + 1) * dh)
            q_h = qc[:, sl].astype(BF16)
            k_h = kc[:, sl]
            k_hb = k_h.astype(BF16)
            v_h = v_ref[0, pl.ds(t0, L), sl]

            log_d = jnp.where(causal, a_col - a_row + i_row, NEG_INF)
            log_inter = a_col + m_st
            m_row = jnp.maximum(jnp.max(log_d, axis=-1, keepdims=True), log_inter)
            w_intra = jnp.exp(log_d - m_row)
            w_inter = jnp.exp(log_inter - m_row)
            s_qk = _dot(q_h, k_hb, NT_DIMS) * w_intra
            num = _dot(s_qk.astype(BF16), v_h) + w_inter * _dot(q_h, ct.astype(BF16))
            qn = jnp.sum(q_h.astype(F32) * n_row, axis=-1, keepdims=True)
            den = jnp.sum(s_qk, axis=-1, keepdims=True) + w_inter * qn
            h = num / jnp.maximum(jnp.abs(den), jnp.exp(-m_row))

            a_end = a_col[L - 1:L, :]
            log_w = a_end - a_col + i_col
            m_new = jnp.maximum(a_end + m_st, jnp.max(log_w, axis=0, keepdims=True))
            w_s = jnp.exp(log_w - m_new)
            decay = jnp.exp(a_end + m_st - m_new)
            kw = k_h * w_s
            ct_new = decay * ct + _dot(kw.astype(BF16), v_h, TN_DIMS)
            n_new = decay * n_row + jnp.sum(kw, axis=0, keepdims=True)
            new_carry.append((ct_new, n_new, m_new))

            ms = jnp.mean(h * h, axis=-1, keepdims=True)
            og = _sigmoid(o_ref[0, pl.ds(t0, L), sl].astype(F32))
            y = h * lax.rsqrt(ms + EPS) * gout_ref[0, hh:hh + 1, :] * og
            y_ref[0, pl.ds(t0, L), sl] = y.astype(y_ref.dtype)
        return tuple(new_carry)

    init = tuple((jnp.zeros((dh, dh), F32), jnp.zeros((1, dh), F32), jnp.zeros((1, 1), F32))
                 for _ in range(heads_per_step))
    lax.fori_loop(0, s // L, chunk, init)


def _mlstm(p, gates, conv_w, b_i, b_f, g_out, col0, n_heads, dh, heads_per_step):
    bsz, s, _ = p.shape
    w = heads_per_step * dh
    nstep = n_heads // heads_per_step
    width = n_heads * dh
    cb = col0 // w
    wb = width // w
    conv_width = conv_w.shape[0]
    bias = jnp.concatenate([b_i, b_f]).reshape(1, 2 * n_heads)
    lane = jnp.arange(2 * n_heads) < n_heads
    big = jnp.where(lane, bias, 0.0)
    bfg = jnp.where(lane, 0.0, bias)
    kern = functools.partial(_ml_kernel, heads_per_step=heads_per_step, dh=dh, conv_width=conv_width)
    seq = lambda off: pl.BlockSpec((1, s, w), lambda b, h: (b, 0, cb + off + h))
    return pl.pallas_call(
        kern,
        grid=(bsz, nstep),
        in_specs=[seq(0), seq(wb), seq(2 * wb), seq(3 * wb),
                  pl.BlockSpec((1, s, 2 * n_heads), lambda b, h: (b, 0, 0)),
                  pl.BlockSpec((conv_width, w), lambda b, h: (0, h)),
                  pl.BlockSpec((conv_width, w), lambda b, h: (0, wb + h)),
                  pl.BlockSpec((1, 2 * n_heads), lambda b, h: (0, 0)),
                  pl.BlockSpec((1, 2 * n_heads), lambda b, h: (0, 0)),
                  pl.BlockSpec((1, heads_per_step, dh), lambda b, h: (h, 0, 0))],
        out_specs=pl.BlockSpec((1, s, w), lambda b, h: (b, 0, h)),
        out_shape=jax.ShapeDtypeStruct((bsz, s, width), BF16),
        scratch_shapes=[pltpu.VMEM((s + 8, w), F32), pltpu.VMEM((s + 8, w), F32)],
        compiler_params=_params("parallel", "parallel"),
    )(p, p, p, p, gates, conv_w, conv_w, big, bfg,
      g_out.reshape(nstep, heads_per_step, dh))


def _outproj_kernel(ysb_ref, yml_ref, x_ref, mod_ref, g_ref, wo_ref, wq_ref, keys_ref,
                    x1_ref, h2_ref, sc_ref):
    wsb = ysb_ref.shape[2]
    mix = _dot(ysb_ref[0], wo_ref[0:wsb, :]) + _dot(yml_ref[0], wo_ref[wsb:, :])
    x1 = x_ref[0] + mod_ref[0, 2:3, :] * mix
    x1_ref[0] = x1
    h2 = _modulated_norm(x1, g_ref[...], mod_ref[0, 4:5, :], mod_ref[0, 3:4, :]).astype(BF16)
    h2_ref[0] = h2
    half = keys_ref.shape[2]
    for i in range(keys_ref.shape[0]):
        qb = _dot(h2, wq_ref[:, i * half:(i + 1) * half]).astype(BF16)
        sc_ref[i, 0] = _dot(keys_ref[i], qb, NT_DIMS)


def _outproj(ysb, yml, x, mod, g, w_out, w_q, keys, tm):
    bsz, s, d = x.shape
    nsk, nk, half = keys.shape
    return pl.pallas_call(
        _outproj_kernel,
        grid=(bsz, s // tm),
        in_specs=[pl.BlockSpec((1, tm, ysb.shape[2]), lambda b, i: (b, i, 0)),
                  pl.BlockSpec((1, tm, yml.shape[2]), lambda b, i: (b, i, 0)),
                  pl.BlockSpec((1, tm, d), lambda b, i: (b, i, 0)),
                  pl.BlockSpec((1, N_MOD, d), lambda b, i: (b, 0, 0)),
                  pl.BlockSpec((1, d), lambda b, i: (0, 0)),
                  pl.BlockSpec(w_out.shape, lambda b, i: (0, 0)),
                  pl.BlockSpec(w_q.shape, lambda b, i: (0, 0)),
                  pl.BlockSpec(keys.shape, lambda b, i: (0, 0, 0))],
        out_specs=[pl.BlockSpec((1, tm, d), lambda b, i: (b, i, 0)),
                   pl.BlockSpec((1, tm, d), lambda b, i: (b, i, 0)),
                   pl.BlockSpec((nsk, 1, nk, tm), lambda b, i: (0, b, 0, i))],
        out_shape=[jax.ShapeDtypeStruct((bsz, s, d), F32),
                   jax.ShapeDtypeStruct((bsz, s, d), BF16),
                   jax.ShapeDtypeStruct((nsk, bsz, nk, s), F32)],
        compiler_params=_params("parallel", "parallel"),
    )(ysb, yml, x, mod, g, w_out, w_q, keys)


def _extract_top(x, k, pos):
    n = x.shape[0]
    rank = jnp.full(x.shape, float(k), F32)
    tops = []
    for a in range(k):
        m = jnp.max(x, axis=0, keepdims=True)
        first = jnp.min(jnp.where(x == m, pos, float(n)), axis=0, keepdims=True)
        pick = pos == first
        rank = jnp.where(pick, float(a), rank)
        x = jnp.where(pick, NEG_INF, x)
        tops.append(m)
    return tops, rank


SUBLANES = 8
ROUTE_ROWS = PEER_TOPK + SUBLANES * (PEER_TOPK // 2 - 1) + PEER_TOPK // 2


def _route_kernel(sc_ref, cnt_ref, c0_ref, rank_ref, e1_ref, top_ref, cand_ref, sel_ref):
    n_heads = cnt_ref.shape[0]
    nk, tb = sc_ref.shape[2], sc_ref.shape[3]
    K = PEER_TOPK
    assert K == 2 * SUBLANES
    pos = lax.broadcasted_iota(jnp.int32, (nk, tb), 0).astype(F32)
    cpos = lax.broadcasted_iota(jnp.int32, (ROUTE_ROWS, tb), 0).astype(F32)
    sub = lax.broadcasted_iota(jnp.int32, (SUBLANES, tb), 0)
    group = lambda a: slice(K + SUBLANES * (a - 1), K + SUBLANES * a)
    tail = slice(K + SUBLANES * (K // 2 - 1), ROUTE_ROWS)

    def head(h, carry):
        s0 = sc_ref[2 * h, 0]
        s1 = sc_ref[2 * h + 1, 0]
        tops0, rank0 = _extract_top(s0, K, pos)
        tops1, rank1 = _extract_top(s1, K, pos)
        for a in range(K):
            top_ref[0, a:a + 1, :] = tops0[a]
            top_ref[1, a:a + 1, :] = tops1[a]

        cand_ref[0:K, :] = tops0[0] + top_ref[1]
        head1 = top_ref[1, 0:SUBLANES, :]
        for a in range(1, K // 2):
            cand_ref[group(a), :] = jnp.where(sub < K // (a + 1), tops0[a] + head1, NEG_INF)
        cand_ref[tail, :] = top_ref[0, K // 2:K, :] + tops1[0]
        cand = cand_ref[...]

        _, pick_round = _extract_top(cand, K, cpos)
        sel = jnp.where(pick_round < float(K), 1.0, 0.0)
        z = jnp.sum(sel * jnp.exp(cand - (tops0[0] + tops1[0])), axis=0, keepdims=True)
        sel_ref[...] = sel

        cnt = jnp.where(rank0 == 0.0, jnp.sum(sel_ref[0:K, :], axis=0, keepdims=True), 0.0)
        for a in range(1, K // 2):
            cnt = jnp.where(rank0 == float(a), jnp.sum(sel_ref[group(a), :], axis=0, keepdims=True), cnt)
        for a in range(K // 2, K):
            row = tail.start + a - K // 2
            cnt = jnp.where(rank0 == float(a), sel_ref[row:row + 1, :], cnt)
        cnt_ref[h, 0] = cnt
        c0_ref[h, 0] = jnp.exp(s0 - tops0[0]) / z
        rank_ref[h, 0] = rank1.astype(rank_ref.dtype)
        e1_ref[h, 0] = jnp.exp(s1 - tops1[0]).astype(e1_ref.dtype)
        return carry

    lax.fori_loop(0, n_heads, head, 0)


def _route(scores, tb):
    nsk, bsz, nk, s = scores.shape
    n_heads = nsk // 2
    shape = (n_heads, bsz, nk, s)
    spec = pl.BlockSpec((n_heads, 1, nk, tb), lambda b, i: (0, b, 0, i))
    return pl.pallas_call(
        _route_kernel,
        grid=(bsz, s // tb),
        in_specs=[pl.BlockSpec((nsk, 1, nk, tb), lambda b, i: (0, b, 0, i))],
        out_specs=[spec] * 4,
        out_shape=[jax.ShapeDtypeStruct(shape, F32)] * 4,
        scratch_shapes=[pltpu.VMEM((2, PEER_TOPK, tb), F32),
                        pltpu.VMEM((ROUTE_ROWS, tb), F32),
                        pltpu.VMEM((ROUTE_ROWS, tb), F32)],
        compiler_params=_params("parallel", "parallel"),
    )(scores)


def _gelu(x):
    return 0.5 * x * (1.0 + lax.erf(x * math.sqrt(0.5)))


def _peer_kernel(h2_ref, u_ref, vt_ref, cnt_ref, c0_ref, rank_ref, e1_ref, x1_ref, mod_ref,
                 o_ref, acc_ref, act_ref, coef_ref, rank_s, e1_s, *, groups):
    j = pl.program_id(2)
    n_heads, _, nk, tb = cnt_ref.shape
    rows = BF16_SUBLANES

    @pl.when(j == 0)
    def _():
        acc_ref[...] = jnp.zeros_like(acc_ref)
        for h in range(n_heads):
            rank_s[h] = rank_ref[h, 0].astype(BF16)
            e1_s[h] = e1_ref[h, 0].astype(BF16)

    act_ref[...] = _dot(u_ref[...], h2_ref[0], NT_DIMS)
    for gg in range(groups):
        g = j * groups + gg
        n0_rows = [cnt_ref[h, 0, pl.ds(g, 1), :] for h in range(n_heads)]
        c0_rows = [c0_ref[h, 0, pl.ds(g, 1), :] for h in range(n_heads)]
        for l0 in range(0, tb, LANES):
            ln = slice(l0, l0 + LANES)
            bcast = lambda row: jnp.broadcast_to(row[:, ln], (rows, LANES)).astype(BF16)
            n0 = [bcast(row) for row in n0_rows]
            c0 = [bcast(row) for row in c0_rows]
            for r0 in range(0, nk, rows):
                rw = slice(r0, r0 + rows)
                cf = jnp.zeros((rows, LANES), BF16)
                for h in range(n_heads):
                    cf = cf + jnp.where(rank_s[h, rw, ln] < n0[h], e1_s[h, rw, ln] * c0[h], 0.0)
                er = slice(gg * nk + r0, gg * nk + r0 + rows)
                coef_ref[er, ln] = cf * _gelu(act_ref[er, ln]).astype(BF16)
    acc_ref[...] += _dot(vt_ref[...], coef_ref[...])

    @pl.when(j == pl.num_programs(2) - 1)
    def _():
        o_ref[0] = x1_ref[0] + mod_ref[0, 5:6, :] * acc_ref[...].T


def _peer(h2, u, vt, cnt, c0, rank, e1, x1, mod, tb, groups):
    bsz, s, d = h2.shape
    n_heads, _, nk, _ = cnt.shape
    te = groups * nk
    n_exp = u.shape[0]
    rspec = pl.BlockSpec((n_heads, 1, nk, tb), lambda b, i, j: (0, b, 0, i))
    return pl.pallas_call(
        functools.partial(_peer_kernel, groups=groups),
        grid=(bsz, s // tb, n_exp // te),
        in_specs=[pl.BlockSpec((1, tb, d), lambda b, i, j: (b, i, 0)),
                  pl.BlockSpec((te, d), lambda b, i, j: (j, 0)),
                  pl.BlockSpec((d, te), lambda b, i, j: (0, j)),
                  rspec, rspec, rspec, rspec,
                  pl.BlockSpec((1, tb, d), lambda b, i, j: (b, i, 0)),
                  pl.BlockSpec((1, N_MOD, d), lambda b, i, j: (b, 0, 0))],
        out_specs=pl.BlockSpec((1, tb, d), lambda b, i, j: (b, i, 0)),
        out_shape=jax.ShapeDtypeStruct((bsz, s, d), F32),
        scratch_shapes=[pltpu.VMEM((d, tb), F32), pltpu.VMEM((te, tb), F32),
                        pltpu.VMEM((te, tb), BF16), pltpu.VMEM((n_heads, nk, tb), BF16),
                        pltpu.VMEM((n_heads, nk, tb), BF16)],
        compiler_params=_params("parallel", "parallel", "arbitrary"),
    )(h2, u, vt, cnt, c0, rank, e1, x1, mod)


def _layer(x, c, w_ada, b_ada, g_norm1, w_in, b_igate, b_fgate, conv_w, g_q_sb, g_k_sb,
           g_out_sb, g_out_ml, w_out, g_norm2, w_q_peer, sub_keys, expert_u, expert_v):
    bsz, s, d = x.shape
    sb_heads, sb_hd = g_out_sb.shape
    ml_heads, ml_hd = g_out_ml.shape
    sb_width = sb_heads * sb_hd
    ml_width = ml_heads * ml_hd
    n_main = 3 * sb_width + 4 * ml_width
    peer_heads, _, n_keys, half = sub_keys.shape

    tm = min(512, s)
    tq = min(256, s)
    tb = min(512, s)

    mod = _ada(c, w_ada, b_ada).reshape(bsz, N_MOD, d)
    p, gates = _inproj(x, mod, g_norm1.reshape(1, d), w_in[:, :n_main].astype(BF16),
                       w_in[:, n_main:].astype(BF16), tm)

    tile2 = lambda g: jnp.concatenate([g, g], axis=-1)
    ysb = _sb_attention(p, tile2(g_q_sb).reshape(1, 2 * sb_hd), tile2(g_k_sb).reshape(1, 2 * sb_hd),
                        g_out_sb.reshape(sb_heads // 2, 1, 2 * sb_hd), sb_heads, sb_hd, tq)
    yml = _mlstm(p, gates, conv_w, b_igate, b_fgate, g_out_ml, 3 * sb_width, ml_heads, ml_hd,
                 heads_per_step=2)

    x1, h2, scores = _outproj(ysb, yml, x, mod, g_norm2.reshape(1, d), w_out.astype(BF16),
                              w_q_peer.astype(BF16),
                              sub_keys.reshape(2 * peer_heads, n_keys, half).astype(BF16),
                              min(256, s))
    cnt, c0, rank, e1 = _route(scores, min(256, s))
    return _peer(h2, expert_u.astype(BF16), expert_v.T.astype(BF16), cnt, c0, rank, e1, x1, mod,
                 tb, groups=4)


def kernel(x, c, w_ada, b_ada, g_norm1, w_in, b_igate, b_fgate, conv_w, g_q_sb, g_k_sb, g_out_sb,
           g_out_ml, w_out, g_norm2, w_q_peer, sub_keys, expert_u, expert_v):
    params = (w_ada, b_ada, g_norm1, w_in, b_igate, b_fgate, conv_w, g_q_sb, g_k_sb, g_out_sb,
              g_out_ml, w_out, g_norm2, w_q_peer, sub_keys, expert_u, expert_v)
    for layer in range(w_ada.shape[0]):
        x = _layer(x, c, *(t[layer] for t in params))
    return x
```

```python
import functools
import math

import jax
import jax.numpy as jnp
from jax import lax
from jax.experimental import pallas as pl
from jax.experimental.pallas import tpu as pltpu

F32 = jnp.float32
BF16 = jnp.bfloat16

EPS = 1e-6
N_MOD = 6
MLSTM_CHUNK = 64
PEER_TOPK = 16
NEG_INF = float("-inf")
F32_EXP_UNDERFLOW = math.log(2.0 ** -126)

V7X_VMEM_BYTES = 64 * 1024 * 1024
VMEM_LIMIT = V7X_VMEM_BYTES - 12 * 1024 * 1024
LANES = 128
BF16_SUBLANES = 16

NT_DIMS = (((1,), (1,)), ((), ()))
NN_DIMS = (((1,), (0,)), ((), ()))
TN_DIMS = (((0,), (0,)), ((), ()))


def _params(*sem):
    return pltpu.CompilerParams(dimension_semantics=sem, vmem_limit_bytes=VMEM_LIMIT)


def _dot(a, b, dims=NN_DIMS):
    return lax.dot_general(a, b, dims, preferred_element_type=F32)


def _split_bf16(a):
    hi = a.astype(BF16)
    lo = (a - hi.astype(F32)).astype(BF16)
    return hi, lo


def _dot_f32(a, b, dims=NN_DIMS):
    ah, al = _split_bf16(a)
    bh, bl = _split_bf16(b)
    return _dot(ah, bh, dims) + (_dot(ah, bl, dims) + _dot(al, bh, dims))


def _dot_exact_rhs(a, b_bf16, dims=NN_DIMS):
    ah, al = _split_bf16(a)
    return _dot(ah, b_bf16, dims) + _dot(al, b_bf16, dims)


def _sigmoid(x):
    return 1.0 / (1.0 + jnp.exp(-x))


def _log_sigmoid_neg(z):
    return -(jnp.maximum(z, 0.0) + jnp.log(1.0 + jnp.exp(-jnp.abs(z))))


def _ada_kernel(c_ref, w_ref, b_ref, o_ref):
    c = c_ref[...]
    o_ref[...] = _dot_f32(c * _sigmoid(c), w_ref[...]) + b_ref[...]


def _ada(c, w, b):
    bsz, d = c.shape
    n = w.shape[1]
    tn = n // N_MOD
    return pl.pallas_call(
        _ada_kernel,
        grid=(n // tn,),
        in_specs=[pl.BlockSpec((bsz, d), lambda j: (0, 0)),
                  pl.BlockSpec((d, tn), lambda j: (0, j)),
                  pl.BlockSpec((1, tn), lambda j: (0, j))],
        out_specs=pl.BlockSpec((bsz, tn), lambda j: (0, j)),
        out_shape=jax.ShapeDtypeStruct((bsz, n), F32),
        compiler_params=_params("parallel"),
    )(c, w, b.reshape(1, n))


def _modulated_norm(x, g, scale, shift):
    y = x * lax.rsqrt(jnp.mean(x * x, axis=-1, keepdims=True) + EPS)
    return (y * g) * (1.0 + scale) + shift


def _inproj_kernel(x_ref, mod_ref, g_ref, w_ref, wg_ref, p_ref, gate_ref, *, n_chunk):
    h = _modulated_norm(x_ref[0], g_ref[...], mod_ref[0, 1:2, :], mod_ref[0, 0:1, :])
    hb = h.astype(BF16)
    for n0 in range(0, w_ref.shape[1], n_chunk):
        p_ref[0, :, n0:n0 + n_chunk] = _dot(hb, w_ref[:, n0:n0 + n_chunk]).astype(BF16)
    gate_ref[0] = _dot(hb, wg_ref[...])


def _inproj(x, mod, g, w_main, w_gate, tm):
    bsz, s, d = x.shape
    n = w_main.shape[1]
    ng = w_gate.shape[1]
    return pl.pallas_call(
        functools.partial(_inproj_kernel, n_chunk=512),
        grid=(bsz, s // tm),
        in_specs=[pl.BlockSpec((1, tm, d), lambda b, i: (b, i, 0)),
                  pl.BlockSpec((1, N_MOD, d), lambda b, i: (b, 0, 0)),
                  pl.BlockSpec((1, d), lambda b, i: (0, 0)),
                  pl.BlockSpec((d, n), lambda b, i: (0, 0)),
                  pl.BlockSpec((d, ng), lambda b, i: (0, 0))],
        out_specs=[pl.BlockSpec((1, tm, n), lambda b, i: (b, i, 0)),
                   pl.BlockSpec((1, tm, ng), lambda b, i: (b, i, 0))],
        out_shape=[jax.ShapeDtypeStruct((bsz, s, n), BF16),
                   jax.ShapeDtypeStruct((bsz, s, ng), F32)],
        compiler_params=_params("parallel", "parallel"),
    )(x, mod, g, w_main, w_gate)


def _pair_rms(x, g2, hd):
    lane = lax.broadcasted_iota(jnp.int32, x.shape, 1)
    sq = x * x
    s0 = jnp.sum(jnp.where(lane < hd, sq, 0.0), axis=-1, keepdims=True)
    s1 = jnp.sum(jnp.where(lane < hd, 0.0, sq), axis=-1, keepdims=True)
    inv = jnp.where(lane < hd, lax.rsqrt(s0 / hd + EPS), lax.rsqrt(s1 / hd + EPS))
    return x * inv * g2


def _sb_kernel(q_ref, k_ref, v_ref, gq_ref, gk_ref, go_ref, o_ref, kn_ref, acc_ref, run_ref,
               *, tq, hd):
    qi = pl.program_id(2)
    s = k_ref.shape[1]
    rows = 512 if s % 512 == 0 else tq

    @pl.when(qi == 0)
    def _():
        for r0 in range(0, s, rows):
            kk = k_ref[0, r0:r0 + rows, :].astype(F32)
            kn_ref[r0:r0 + rows, :] = _pair_rms(kk, gk_ref[...], hd).astype(BF16)

    qn = _pair_rms(q_ref[0].astype(F32), gq_ref[...], hd) * (hd ** -0.5)
    lane = lax.broadcasted_iota(jnp.int32, qn.shape, 1)
    q_heads = (jnp.where(lane < hd, qn, 0.0).astype(BF16),
               jnp.where(lane < hd, 0.0, qn).astype(BF16))

    r = lax.broadcasted_iota(jnp.int32, (tq, tq), 0)
    c = lax.broadcasted_iota(jnp.int32, (tq, tq), 1)
    suffix = jnp.where(r >= c, 1.0, 0.0).astype(BF16)
    visible = c < r

    def block(j, diagonal):
        k0 = pl.multiple_of(j * tq, tq)
        kb = kn_ref[pl.ds(k0, tq), :]
        vb = v_ref[0, pl.ds(k0, tq), :]
        top = []
        for h in range(2):
            z = _dot(q_heads[h], kb, NT_DIMS)
            lk = _log_sigmoid_neg(z)
            if diagonal:
                lk = jnp.where(visible, lk, 0.0)
            cs = _dot_exact_rhs(lk, suffix)
            run = run_ref[h]
            w = jnp.exp(z + cs + run)
            if diagonal:
                w = jnp.where(visible, w, 0.0)
            acc_ref[h] += _dot(w.astype(BF16), vb)
            run = run + cs[:, 0:1]
            run_ref[h] = run
            top.append(jnp.max(run))
        return jnp.maximum(top[0], top[1])

    acc_ref[...] = jnp.zeros_like(acc_ref)
    run_ref[...] = jnp.zeros_like(run_ref)
    z_bound = 1.02 * (hd ** 0.5) * jnp.max(jnp.abs(gq_ref[...])) * jnp.max(jnp.abs(gk_ref[...]))

    def more(state):
        j, top = state
        return jnp.logical_and(j >= 0, top + z_bound > F32_EXP_UNDERFLOW)

    lax.while_loop(more, lambda st: (st[0] - 1, block(st[0], False)), (qi - 1, block(qi, True)))

    y = jnp.where(lane < hd, acc_ref[0], acc_ref[1])
    o_ref[0] = _pair_rms(y, go_ref[0], hd).astype(o_ref.dtype)


def _sb_attention(p, g_q2, g_k2, g_out2, n_heads, hd, tq):
    bsz, s, _ = p.shape
    w = 2 * hd
    npair = n_heads // 2
    return pl.pallas_call(
        functools.partial(_sb_kernel, tq=tq, hd=hd),
        grid=(bsz, npair, s // tq),
        in_specs=[pl.BlockSpec((1, tq, w), lambda b, h, i: (b, i, h)),
                  pl.BlockSpec((1, s, w), lambda b, h, i: (b, 0, npair + h)),
                  pl.BlockSpec((1, s, w), lambda b, h, i: (b, 0, 2 * npair + h)),
                  pl.BlockSpec((1, w), lambda b, h, i: (0, 0)),
                  pl.BlockSpec((1, w), lambda b, h, i: (0, 0)),
                  pl.BlockSpec((1, 1, w), lambda b, h, i: (h, 0, 0))],
        out_specs=pl.BlockSpec((1, tq, w), lambda b, h, i: (b, i, h)),
        out_shape=jax.ShapeDtypeStruct((bsz, s, n_heads * hd), BF16),
        scratch_shapes=[pltpu.VMEM((s, w), BF16), pltpu.VMEM((2, tq, w), F32),
                        pltpu.VMEM((2, tq, 1), F32)],
        compiler_params=_params("parallel", "parallel", "arbitrary"),
    )(p, p, p, g_q2, g_k2, g_out2)


def _ml_kernel(q_ref, k_ref, v_ref, o_ref, gate_ref, cwq_ref, cwk_ref, big_ref, bfg_ref,
               gout_ref, y_ref, qpad_ref, kpad_ref, *, heads_per_step, dh, conv_width):
    L = MLSTM_CHUNK
    s = q_ref.shape[1]
    n_heads = gate_ref.shape[2] // 2
    hp = pl.program_id(1)
    halo = 8
    rows = 512 if s % 512 == 0 else L

    for src, dst in ((q_ref, qpad_ref), (k_ref, kpad_ref)):
        dst[0:halo, :] = jnp.zeros((halo, dst.shape[1]), F32)
        for r0 in range(0, s, rows):
            dst[halo + r0:halo + r0 + rows, :] = src[0, r0:r0 + rows, :].astype(F32)

    r = lax.broadcasted_iota(jnp.int32, (L, L), 0)
    c = lax.broadcasted_iota(jnp.int32, (L, L), 1)
    eye = r == c
    causal = c <= r
    prefix = jnp.where(causal, 1.0, 0.0).astype(BF16)
    glane = lax.broadcasted_iota(jnp.int32, (1, 2 * n_heads), 1)

    def conv_silu(pad_ref, w_ref, t0):
        ext = pad_ref[pl.ds(t0, L + halo), :]
        y = jnp.zeros((L, ext.shape[1]), F32)
        for j in range(conv_width):
            off = halo - (conv_width - 1) + j
            y = y + ext[off:off + L, :] * w_ref[j:j + 1, :]
        return y * _sigmoid(y)

    def to_row(col):
        return jnp.sum(jnp.where(eye, col, 0.0), axis=0, keepdims=True)

    def chunk(ci, carry):
        t0 = pl.multiple_of(ci * L, L)
        g = gate_ref[0, pl.ds(t0, L), :]
        i_all = g + big_ref[...]
        lf_hi, lf_lo = _split_bf16(_log_sigmoid_neg(-(g + bfg_ref[...])))
        a_all = _dot(prefix, lf_hi) + _dot(prefix, lf_lo)
        qc = conv_silu(qpad_ref, cwq_ref, t0)
        kc = conv_silu(kpad_ref, cwk_ref, t0) * (dh ** -0.5)
        new_carry = []
        for hh in range(heads_per_step):
            ct, n_row, m_st = carry[hh]
            col = hp * heads_per_step + hh
            pick_i = glane == col
            pick_f = glane == (n_heads + col)
            i_col = jnp.sum(jnp.where(pick_i, i_all, 0.0), axis=1, keepdims=True)
            a_col = jnp.sum(jnp.where(pick_f, a_all, 0.0), axis=1, keepdims=True)
            a_row = to_row(a_col)
            i_row = to_row(i_col)
            sl = slice(hh * dh, (hh + 1) * dh)
            q_h = qc[:, sl].astype(BF16)
            k_h = kc[:, sl]
            k_hb = k_h.astype(BF16)
            v_h = v_ref[0, pl.ds(t0, L), sl]

            log_d = jnp.where(causal, a_col - a_row + i_row, NEG_INF)
            log_inter = a_col + m_st
            m_row = jnp.maximum(jnp.max(log_d, axis=-1, keepdims=True), log_inter)
            w_intra = jnp.exp(log_d - m_row)
            w_inter = jnp.exp(log_inter - m_row)
            s_qk = _dot(q_h, k_hb, NT_DIMS) * w_intra
            num = _dot(s_qk.astype(BF16), v_h) + w_inter * _dot(q_h, ct.astype(BF16))
            qn = jnp.sum(q_h.astype(F32) * n_row, axis=-1, keepdims=True)
            den = jnp.sum(s_qk, axis=-1, keepdims=True) + w_inter * qn
            h = num / jnp.maximum(jnp.abs(den), jnp.exp(-m_row))

            a_end = a_col[L - 1:L, :]
            log_w = a_end - a_col + i_col
            m_new = jnp.maximum(a_end + m_st, jnp.max(log_w, axis=0, keepdims=True))
            w_s = jnp.exp(log_w - m_new)
            decay = jnp.exp(a_end + m_st - m_new)
            kw = k_h * w_s
            ct_new = decay * ct + _dot(kw.astype(BF16), v_h, TN_DIMS)
            n_new = decay * n_row + jnp.sum(kw, axis=0, keepdims=True)
            new_carry.append((ct_new, n_new, m_new))

            ms = jnp.mean(h * h, axis=-1, keepdims=True)
            og = _sigmoid(o_ref[0, pl.ds(t0, L), sl].astype(F32))
            y = h * lax.rsqrt(ms + EPS) * gout_ref[0, hh:hh + 1, :] * og
            y_ref[0, pl.ds(t0, L), sl] = y.astype(y_ref.dtype)
        return tuple(new_carry)

    init = tuple((jnp.zeros((dh, dh), F32), jnp.zeros((1, dh), F32), jnp.zeros((1, 1), F32))
                 for _ in range(heads_per_step))
    lax.fori_loop(0, s // L, chunk, init)


def _mlstm(p, gates, conv_w, b_i, b_f, g_out, col0, n_heads, dh, heads_per_step):
    bsz, s, _ = p.shape
    w = heads_per_step * dh
    nstep = n_heads // heads_per_step
    width = n_heads * dh
    cb = col0 // w
    wb = width // w
    conv_width = conv_w.shape[0]
    bias = jnp.concatenate([b_i, b_f]).reshape(1, 2 * n_heads)
    lane = jnp.arange(2 * n_heads) < n_heads
    big = jnp.where(lane, bias, 0.0)
    bfg = jnp.where(lane, 0.0, bias)
    kern = functools.partial(_ml_kernel, heads_per_step=heads_per_step, dh=dh, conv_width=conv_width)
    seq = lambda off: pl.BlockSpec((1, s, w), lambda b, h: (b, 0, cb + off + h))
    return pl.pallas_call(
        kern,
        grid=(bsz, nstep),
        in_specs=[seq(0), seq(wb), seq(2 * wb), seq(3 * wb),
                  pl.BlockSpec((1, s, 2 * n_heads), lambda b, h: (b, 0, 0)),
                  pl.BlockSpec((conv_width, w), lambda b, h: (0, h)),
                  pl.BlockSpec((conv_width, w), lambda b, h: (0, wb + h)),
                  pl.BlockSpec((1, 2 * n_heads), lambda b, h: (0, 0)),
                  pl.BlockSpec((1, 2 * n_heads), lambda b, h: (0, 0)),
                  pl.BlockSpec((1, heads_per_step, dh), lambda b, h: (h, 0, 0))],
        out_specs=pl.BlockSpec((1, s, w), lambda b, h: (b, 0, h)),
        out_shape=jax.ShapeDtypeStruct((bsz, s, width), BF16),
        scratch_shapes=[pltpu.VMEM((s + 8, w), F32), pltpu.VMEM((s + 8, w), F32)],
        compiler_params=_params("parallel", "parallel"),
    )(p, p, p, p, gates, conv_w, conv_w, big, bfg,
      g_out.reshape(nstep, heads_per_step, dh))


def _outproj_kernel(ysb_ref, yml_ref, x_ref, mod_ref, g_ref, wo_ref, wq_ref, keys_ref,
                    x1_ref, h2_ref, sc_ref):
    wsb = ysb_ref.shape[2]
    mix = _dot(ysb_ref[0], wo_ref[0:wsb, :]) + _dot(yml_ref[0], wo_ref[wsb:, :])
    x1 = x_ref[0] + mod_ref[0, 2:3, :] * mix
    x1_ref[0] = x1
    h2 = _modulated_norm(x1, g_ref[...], mod_ref[0, 4:5, :], mod_ref[0, 3:4, :]).astype(BF16)
    h2_ref[0] = h2
    half = keys_ref.shape[2]
    for i in range(keys_ref.shape[0]):
        qb = _dot(h2, wq_ref[:, i * half:(i + 1) * half]).astype(BF16)
        sc_ref[i, 0] = _dot(keys_ref[i], qb, NT_DIMS)


def _outproj(ysb, yml, x, mod, g, w_out, w_q, keys, tm):
    bsz, s, d = x.shape
    nsk, nk, half = keys.shape
    return pl.pallas_call(
        _outproj_kernel,
        grid=(bsz, s // tm),
        in_specs=[pl.BlockSpec((1, tm, ysb.shape[2]), lambda b, i: (b, i, 0)),
                  pl.BlockSpec((1, tm, yml.shape[2]), lambda b, i: (b, i, 0)),
                  pl.BlockSpec((1, tm, d), lambda b, i: (b, i, 0)),
                  pl.BlockSpec((1, N_MOD, d), lambda b, i: (b, 0, 0)),
                  pl.BlockSpec((1, d), lambda b, i: (0, 0)),
                  pl.BlockSpec(w_out.shape, lambda b, i: (0, 0)),
                  pl.BlockSpec(w_q.shape, lambda b, i: (0, 0)),
                  pl.BlockSpec(keys.shape, lambda b, i: (0, 0, 0))],
        out_specs=[pl.BlockSpec((1, tm, d), lambda b, i: (b, i, 0)),
                   pl.BlockSpec((1, tm, d), lambda b, i: (b, i, 0)),
                   pl.BlockSpec((nsk, 1, nk, tm), lambda b, i: (0, b, 0, i))],
        out_shape=[jax.ShapeDtypeStruct((bsz, s, d), F32),
                   jax.ShapeDtypeStruct((bsz, s, d), BF16),
                   jax.ShapeDtypeStruct((nsk, bsz, nk, s), F32)],
        compiler_params=_params("parallel", "parallel"),
    )(ysb, yml, x, mod, g, w_out, w_q, keys)


def _extract_top(x, k, pos):
    n = x.shape[0]
    rank = jnp.full(x.shape, float(k), F32)
    tops = []
    for a in range(k):
        m = jnp.max(x, axis=0, keepdims=True)
        first = jnp.min(jnp.where(x == m, pos, float(n)), axis=0, keepdims=True)
        pick = pos == first
        rank = jnp.where(pick, float(a), rank)
        x = jnp.where(pick, NEG_INF, x)
        tops.append(m)
    return tops, rank


SUBLANES = 8
COUNT_BITS = 5
COUNT_MASK = (1 << COUNT_BITS) - 1
ROUTE_ROWS = PEER_TOPK + SUBLANES * (PEER_TOPK // 2 - 1) + PEER_TOPK // 2


def _route_kernel(sc_ref, cc_ref, rank_ref, e1_ref, top_ref, cand_ref, sel_ref):
    n_heads = cc_ref.shape[0]
    nk, tb = sc_ref.shape[2], sc_ref.shape[3]
    K = PEER_TOPK
    assert K == 2 * SUBLANES
    pos = lax.broadcasted_iota(jnp.int32, (nk, tb), 0).astype(F32)
    cpos = lax.broadcasted_iota(jnp.int32, (ROUTE_ROWS, tb), 0).astype(F32)
    sub = lax.broadcasted_iota(jnp.int32, (SUBLANES, tb), 0)
    group = lambda a: slice(K + SUBLANES * (a - 1), K + SUBLANES * a)
    tail = slice(K + SUBLANES * (K // 2 - 1), ROUTE_ROWS)

    def head(h, carry):
        s0 = sc_ref[2 * h, 0]
        s1 = sc_ref[2 * h + 1, 0]
        tops0, rank0 = _extract_top(s0, K, pos)
        tops1, rank1 = _extract_top(s1, K, pos)
        for a in range(K):
            top_ref[0, a:a + 1, :] = tops0[a]
            top_ref[1, a:a + 1, :] = tops1[a]

        cand_ref[0:K, :] = tops0[0] + top_ref[1]
        head1 = top_ref[1, 0:SUBLANES, :]
        for a in range(1, K // 2):
            cand_ref[group(a), :] = jnp.where(sub < K // (a + 1), tops0[a] + head1, NEG_INF)
        cand_ref[tail, :] = top_ref[0, K // 2:K, :] + tops1[0]
        cand = cand_ref[...]

        _, pick_round = _extract_top(cand, K, cpos)
        sel = jnp.where(pick_round < float(K), 1.0, 0.0)
        z = jnp.sum(sel * jnp.exp(cand - (tops0[0] + tops1[0])), axis=0, keepdims=True)
        sel_ref[...] = sel

        cnt = jnp.where(rank0 == 0.0, jnp.sum(sel_ref[0:K, :], axis=0, keepdims=True), 0.0)
        for a in range(1, K // 2):
            cnt = jnp.where(rank0 == float(a), jnp.sum(sel_ref[group(a), :], axis=0, keepdims=True), cnt)
        for a in range(K // 2, K):
            row = tail.start + a - K // 2
            cnt = jnp.where(rank0 == float(a), sel_ref[row:row + 1, :], cnt)
        c0_bits = pltpu.bitcast(jnp.exp(s0 - tops0[0]) / z, jnp.int32)
        cc_ref[h, 0] = pltpu.bitcast((c0_bits & ~COUNT_MASK) | cnt.astype(jnp.int32), F32)
        rank_ref[h, 0] = pltpu.bitcast(rank1.astype(BF16), jnp.uint32)
        e1_ref[h, 0] = pltpu.bitcast(jnp.exp(s1 - tops1[0]).astype(BF16), jnp.uint32)
        return carry

    lax.fori_loop(0, n_heads, head, 0)


def _route(scores, tb):
    nsk, bsz, nk, s = scores.shape
    n_heads = nsk // 2
    spec = lambda rows: pl.BlockSpec((n_heads, 1, rows, tb), lambda b, i: (0, b, 0, i))
    words = jax.ShapeDtypeStruct((n_heads, bsz, nk // 2, s), jnp.uint32)
    return pl.pallas_call(
        _route_kernel,
        grid=(bsz, s // tb),
        in_specs=[pl.BlockSpec((nsk, 1, nk, tb), lambda b, i: (0, b, 0, i))],
        out_specs=[spec(nk), spec(nk // 2), spec(nk // 2)],
        out_shape=[jax.ShapeDtypeStruct((n_heads, bsz, nk, s), F32), words, words],
        scratch_shapes=[pltpu.VMEM((2, PEER_TOPK, tb), F32),
                        pltpu.VMEM((ROUTE_ROWS, tb), F32),
                        pltpu.VMEM((ROUTE_ROWS, tb), F32)],
        compiler_params=_params("parallel", "parallel"),
    )(scores)


def _gelu(x):
    return 0.5 * x * (1.0 + lax.erf(x * math.sqrt(0.5)))


def _peer_kernel(h2_ref, u_ref, vta_ref, vtb_ref, cc_ref, rank_ref, e1_ref, o_ref, act_ref, coef_ref,
                 *, groups):
    j = pl.program_id(2)
    n_steps = pl.num_programs(2) - 1
    n_heads, _, nk, tb = cc_ref.shape
    te = groups * nk
    rows = BF16_SUBLANES
    words = rows // 2

    def activations(slot):
        act_ref[slot] = _dot(u_ref[slot * te:(slot + 1) * te, :], h2_ref[0], NT_DIMS)

    def apply(vt_ref, slot):
        o_ref[0] += _dot(vt_ref[0], coef_ref[slot])

    def coefficients(slot):
        for gg in range(groups):
            g = (2 * j + slot) * groups + gg
            cc_rows = [cc_ref[h, 0, pl.ds(g, 1), :] for h in range(n_heads)]
            for l0 in range(0, tb, LANES):
                ln = slice(l0, l0 + LANES)
                n0, c0 = [], []
                for row in cc_rows:
                    tile = jnp.broadcast_to(row[:, ln], (rows, LANES))
                    count = pltpu.bitcast(tile, jnp.int32) & COUNT_MASK
                    n0.append(count.astype(F32).astype(BF16))
                    c0.append(tile.astype(BF16))
                for r0 in range(0, nk, rows):
                    wr = slice(r0 // 2, r0 // 2 + words)
                    cf = jnp.zeros((rows, LANES), BF16)
                    for h in range(n_heads):
                        rank = pltpu.bitcast(rank_ref[h, 0, wr, ln], BF16)
                        e1 = pltpu.bitcast(e1_ref[h, 0, wr, ln], BF16)
                        cf = cf + jnp.where(rank < n0[h], e1 * c0[h], 0.0)
                    er = slice(gg * nk + r0, gg * nk + r0 + rows)
                    coef_ref[slot, er, ln] = cf * _gelu(act_ref[slot, er, ln]).astype(BF16)

    @pl.when(j == 0)
    def _():
        o_ref[...] = jnp.zeros_like(o_ref)
        coef_ref[1] = jnp.zeros(coef_ref.shape[1:], BF16)

    @pl.when(j < n_steps)
    def _():
        activations(0)
        apply(vtb_ref, 1)
        coefficients(0)
        activations(1)
        apply(vta_ref, 0)
        coefficients(1)

    @pl.when(j == n_steps)
    def _():
        apply(vtb_ref, 1)


def _peer(h2, u, vt3, cc, rank, e1, tb, groups):
    bsz, s, d = h2.shape
    n_heads, _, nk, _ = cc.shape
    te = groups * nk
    n_steps = u.shape[0] // (2 * te)
    wspec = pl.BlockSpec((n_heads, 1, nk // 2, tb), lambda b, i, j: (0, b, 0, i))
    return pl.pallas_call(
        functools.partial(_peer_kernel, groups=groups),
        grid=(bsz, s // tb, n_steps + 1),
        in_specs=[pl.BlockSpec((1, tb, d), lambda b, i, j: (b, i, 0)),
                  pl.BlockSpec((2 * te, d), lambda b, i, j: (jnp.minimum(j, n_steps - 1), 0)),
                  pl.BlockSpec((1, d, te), lambda b, i, j: (jnp.minimum(2 * j, 2 * n_steps - 2), 0, 0)),
                  pl.BlockSpec((1, d, te), lambda b, i, j: (jnp.maximum(2 * j - 1, 0), 0, 0)),
                  pl.BlockSpec((n_heads, 1, nk, tb), lambda b, i, j: (0, b, 0, i)),
                  wspec, wspec],
        out_specs=pl.BlockSpec((1, d, tb), lambda b, i, j: (b, 0, i)),
        out_shape=jax.ShapeDtypeStruct((bsz, d, s), F32),
        scratch_shapes=[pltpu.VMEM((2, te, tb), F32), pltpu.VMEM((2, te, tb), BF16)],
        compiler_params=_params("parallel", "parallel", "arbitrary"),
    )(h2, u, vt3, vt3, cc, rank, e1)


def _residual_kernel(x1_ref, yt_ref, mod_ref, o_ref):
    o_ref[0] = x1_ref[0] + mod_ref[0, 5:6, :] * yt_ref[0].T


def _residual(x1, yt, mod, tm):
    bsz, s, d = x1.shape
    return pl.pallas_call(
        _residual_kernel,
        grid=(bsz, s // tm),
        in_specs=[pl.BlockSpec((1, tm, d), lambda b, i: (b, i, 0)),
                  pl.BlockSpec((1, d, tm), lambda b, i: (b, 0, i)),
                  pl.BlockSpec((1, N_MOD, d), lambda b, i: (b, 0, 0))],
        out_specs=pl.BlockSpec((1, tm, d), lambda b, i: (b, i, 0)),
        out_shape=jax.ShapeDtypeStruct((bsz, s, d), F32),
        compiler_params=_params("parallel", "parallel"),
    )(x1, yt, mod)


def _layer(x, c, w_ada, b_ada, g_norm1, w_in, b_igate, b_fgate, conv_w, g_q_sb, g_k_sb,
           g_out_sb, g_out_ml, w_out, g_norm2, w_q_peer, sub_keys, expert_u, expert_v):
    bsz, s, d = x.shape
    sb_heads, sb_hd = g_out_sb.shape
    ml_heads, ml_hd = g_out_ml.shape
    sb_width = sb_heads * sb_hd
    ml_width = ml_heads * ml_hd
    n_main = 3 * sb_width + 4 * ml_width
    peer_heads, _, n_keys, half = sub_keys.shape

    tm = min(512, s)
    tq = min(256, s)
    tb = min(1024, s)

    mod = _ada(c, w_ada, b_ada).reshape(bsz, N_MOD, d)
    p, gates = _inproj(x, mod, g_norm1.reshape(1, d), w_in[:, :n_main].astype(BF16),
                       w_in[:, n_main:].astype(BF16), tm)

    tile2 = lambda g: jnp.concatenate([g, g], axis=-1)
    ysb = _sb_attention(p, tile2(g_q_sb).reshape(1, 2 * sb_hd), tile2(g_k_sb).reshape(1, 2 * sb_hd),
                        g_out_sb.reshape(sb_heads // 2, 1, 2 * sb_hd), sb_heads, sb_hd, tq)
    yml = _mlstm(p, gates, conv_w, b_igate, b_fgate, g_out_ml, 3 * sb_width, ml_heads, ml_hd,
                 heads_per_step=2)

    x1, h2, scores = _outproj(ysb, yml, x, mod, g_norm2.reshape(1, d), w_out.astype(BF16),
                              w_q_peer.astype(BF16),
                              sub_keys.reshape(2 * peer_heads, n_keys, half).astype(BF16),
                              min(256, s))
    cc, rank, e1 = _route(scores, min(256, s))
    groups = 4
    te = groups * n_keys
    vt3 = expert_v.astype(BF16).reshape(-1, te, d).transpose(0, 2, 1)
    yt = _peer(h2, expert_u.astype(BF16), vt3, cc, rank, e1, tb, groups)
    return _residual(x1, yt, mod, tm)


def kernel(x, c, w_ada, b_ada, g_norm1, w_in, b_igate, b_fgate, conv_w, g_q_sb, g_k_sb, g_out_sb,
           g_out_ml, w_out, g_norm2, w_q_peer, sub_keys, expert_u, expert_v):
    params = (w_ada, b_ada, g_norm1, w_in, b_igate, b_fgate, conv_w, g_q_sb, g_k_sb, g_out_sb,
              g_out_ml, w_out, g_norm2, w_q_peer, sub_keys, expert_u, expert_v)
    for layer in range(w_ada.shape[0]):
        x = _layer(x, c, *(t[layer] for t in params))
    return x
```

```python
import functools
import math

import jax
import jax.numpy as jnp
from jax import lax
from jax.experimental import pallas as pl
from jax.experimental.pallas import tpu as pltpu

F32 = jnp.float32
BF16 = jnp.bfloat16

EPS = 1e-6
N_MOD = 6
MLSTM_CHUNK = 64
PEER_TOPK = 16
NEG_INF = float("-inf")
F32_EXP_UNDERFLOW = math.log(2.0 ** -126)

V7X_VMEM_BYTES = 64 * 1024 * 1024
VMEM_LIMIT = V7X_VMEM_BYTES - 12 * 1024 * 1024
LANES = 128
BF16_SUBLANES = 16

NT_DIMS = (((1,), (1,)), ((), ()))
NN_DIMS = (((1,), (0,)), ((), ()))
TN_DIMS = (((0,), (0,)), ((), ()))


def _params(*sem):
    return pltpu.CompilerParams(dimension_semantics=sem, vmem_limit_bytes=VMEM_LIMIT)


def _dot(a, b, dims=NN_DIMS):
    return lax.dot_general(a, b, dims, preferred_element_type=F32)


def _split_bf16(a):
    hi = a.astype(BF16)
    lo = (a - hi.astype(F32)).astype(BF16)
    return hi, lo


def _dot_f32(a, b, dims=NN_DIMS):
    ah, al = _split_bf16(a)
    bh, bl = _split_bf16(b)
    return _dot(ah, bh, dims) + (_dot(ah, bl, dims) + _dot(al, bh, dims))


def _dot_exact_rhs(a, b_bf16, dims=NN_DIMS):
    ah, al = _split_bf16(a)
    return _dot(ah, b_bf16, dims) + _dot(al, b_bf16, dims)


def _sigmoid(x):
    return 1.0 / (1.0 + jnp.exp(-x))


def _log_sigmoid_neg(z):
    return -(jnp.maximum(z, 0.0) + jnp.log(1.0 + jnp.exp(-jnp.abs(z))))


def _ada_kernel(c_ref, w_ref, b_ref, o_ref):
    c = c_ref[...]
    o_ref[...] = _dot_f32(c * _sigmoid(c), w_ref[...]) + b_ref[...]


def _ada(c, w, b):
    bsz, d = c.shape
    n = w.shape[1]
    tn = n // N_MOD
    return pl.pallas_call(
        _ada_kernel,
        grid=(n // tn,),
        in_specs=[pl.BlockSpec((bsz, d), lambda j: (0, 0)),
                  pl.BlockSpec((d, tn), lambda j: (0, j)),
                  pl.BlockSpec((1, tn), lambda j: (0, j))],
        out_specs=pl.BlockSpec((bsz, tn), lambda j: (0, j)),
        out_shape=jax.ShapeDtypeStruct((bsz, n), F32),
        compiler_params=_params("parallel"),
    )(c, w, b.reshape(1, n))


def _modulated_norm(x, g, scale, shift):
    y = x * lax.rsqrt(jnp.mean(x * x, axis=-1, keepdims=True) + EPS)
    return (y * g) * (1.0 + scale) + shift


def _inproj_kernel(x_ref, mod_ref, g_ref, w_ref, wg_ref, p_ref, gate_ref, *, n_chunk):
    h = _modulated_norm(x_ref[0], g_ref[...], mod_ref[0, 1:2, :], mod_ref[0, 0:1, :])
    hb = h.astype(BF16)
    for n0 in range(0, w_ref.shape[1], n_chunk):
        p_ref[0, :, n0:n0 + n_chunk] = _dot(hb, w_ref[:, n0:n0 + n_chunk]).astype(BF16)
    gate_ref[0] = _dot(hb, wg_ref[...])


def _inproj(x, mod, g, w_main, w_gate, tm):
    bsz, s, d = x.shape
    n = w_main.shape[1]
    ng = w_gate.shape[1]
    return pl.pallas_call(
        functools.partial(_inproj_kernel, n_chunk=512),
        grid=(bsz, s // tm),
        in_specs=[pl.BlockSpec((1, tm, d), lambda b, i: (b, i, 0)),
                  pl.BlockSpec((1, N_MOD, d), lambda b, i: (b, 0, 0)),
                  pl.BlockSpec((1, d), lambda b, i: (0, 0)),
                  pl.BlockSpec((d, n), lambda b, i: (0, 0)),
                  pl.BlockSpec((d, ng), lambda b, i: (0, 0))],
        out_specs=[pl.BlockSpec((1, tm, n), lambda b, i: (b, i, 0)),
                   pl.BlockSpec((1, tm, ng), lambda b, i: (b, i, 0))],
        out_shape=[jax.ShapeDtypeStruct((bsz, s, n), BF16),
                   jax.ShapeDtypeStruct((bsz, s, ng), F32)],
        compiler_params=_params("parallel", "parallel"),
    )(x, mod, g, w_main, w_gate)


def _pair_rms(x, g2, hd):
    lane = lax.broadcasted_iota(jnp.int32, x.shape, 1)
    sq = x * x
    s0 = jnp.sum(jnp.where(lane < hd, sq, 0.0), axis=-1, keepdims=True)
    s1 = jnp.sum(jnp.where(lane < hd, 0.0, sq), axis=-1, keepdims=True)
    inv = jnp.where(lane < hd, lax.rsqrt(s0 / hd + EPS), lax.rsqrt(s1 / hd + EPS))
    return x * inv * g2


def _sb_kernel(q_ref, k_ref, v_ref, gq_ref, gk_ref, go_ref, o_ref, kn_ref, acc_ref, run_ref,
               *, tq, hd):
    qi = pl.program_id(2)
    s = k_ref.shape[1]
    rows = 512 if s % 512 == 0 else tq

    @pl.when(qi == 0)
    def _():
        for r0 in range(0, s, rows):
            kk = k_ref[0, r0:r0 + rows, :].astype(F32)
            kn_ref[r0:r0 + rows, :] = _pair_rms(kk, gk_ref[...], hd).astype(BF16)

    qn = _pair_rms(q_ref[0].astype(F32), gq_ref[...], hd) * (hd ** -0.5)
    lane = lax.broadcasted_iota(jnp.int32, qn.shape, 1)
    q2 = jnp.concatenate([jnp.where(lane < hd, qn, 0.0), jnp.where(lane < hd, 0.0, qn)],
                         axis=0).astype(BF16)

    r = lax.broadcasted_iota(jnp.int32, (tq, tq), 0)
    c = lax.broadcasted_iota(jnp.int32, (tq, tq), 1)
    suffix = jnp.where(r >= c, 1.0, 0.0).astype(BF16)
    r2 = lax.broadcasted_iota(jnp.int32, (2 * tq, tq), 0)
    c2 = lax.broadcasted_iota(jnp.int32, (2 * tq, tq), 1)
    visible = c2 < jnp.where(r2 < tq, r2, r2 - tq)

    def block(j, diagonal):
        k0 = pl.multiple_of(j * tq, tq)
        z = _dot(q2, kn_ref[pl.ds(k0, tq), :], NT_DIMS)
        lk = _log_sigmoid_neg(z)
        if diagonal:
            lk = jnp.where(visible, lk, 0.0)
        cs = _dot_exact_rhs(lk, suffix)
        run = run_ref[...]
        w = jnp.exp(z + cs + run)
        if diagonal:
            w = jnp.where(visible, w, 0.0)
        acc_ref[...] += _dot(w.astype(BF16), v_ref[0, pl.ds(k0, tq), :])
        run = run + cs[:, 0:1]
        run_ref[...] = run
        return jnp.max(run)

    acc_ref[...] = jnp.zeros_like(acc_ref)
    run_ref[...] = jnp.zeros_like(run_ref)
    z_bound = 1.02 * (hd ** 0.5) * jnp.max(jnp.abs(gq_ref[...])) * jnp.max(jnp.abs(gk_ref[...]))

    def more(state):
        j, top = state
        return jnp.logical_and(j >= 0, top + z_bound > F32_EXP_UNDERFLOW)

    lax.while_loop(more, lambda st: (st[0] - 1, block(st[0], False)), (qi - 1, block(qi, True)))

    y = jnp.where(lane < hd, acc_ref[0:tq, :], acc_ref[tq:2 * tq, :])
    o_ref[0] = _pair_rms(y, go_ref[0], hd).astype(o_ref.dtype)


def _sb_attention(p, g_q2, g_k2, g_out2, n_heads, hd, tq):
    bsz, s, _ = p.shape
    w = 2 * hd
    npair = n_heads // 2
    return pl.pallas_call(
        functools.partial(_sb_kernel, tq=tq, hd=hd),
        grid=(bsz, npair, s // tq),
        in_specs=[pl.BlockSpec((1, tq, w), lambda b, h, i: (b, i, h)),
                  pl.BlockSpec((1, s, w), lambda b, h, i: (b, 0, npair + h)),
                  pl.BlockSpec((1, s, w), lambda b, h, i: (b, 0, 2 * npair + h)),
                  pl.BlockSpec((1, w), lambda b, h, i: (0, 0)),
                  pl.BlockSpec((1, w), lambda b, h, i: (0, 0)),
                  pl.BlockSpec((1, 1, w), lambda b, h, i: (h, 0, 0))],
        out_specs=pl.BlockSpec((1, tq, w), lambda b, h, i: (b, i, h)),
        out_shape=jax.ShapeDtypeStruct((bsz, s, n_heads * hd), BF16),
        scratch_shapes=[pltpu.VMEM((s, w), BF16), pltpu.VMEM((2 * tq, w), F32),
                        pltpu.VMEM((2 * tq, 1), F32)],
        compiler_params=_params("parallel", "parallel", "arbitrary"),
    )(p, p, p, g_q2, g_k2, g_out2)


def _ml_kernel(q_ref, k_ref, v_ref, o_ref, gate_ref, cwq_ref, cwk_ref, big_ref, bfg_ref,
               gout_ref, y_ref, qpad_ref, kpad_ref, *, heads_per_step, dh, conv_width):
    L = MLSTM_CHUNK
    s = q_ref.shape[1]
    n_heads = gate_ref.shape[2] // 2
    hp = pl.program_id(1)
    halo = 8
    rows = 512 if s % 512 == 0 else L

    for src, dst in ((q_ref, qpad_ref), (k_ref, kpad_ref)):
        dst[0:halo, :] = jnp.zeros((halo, dst.shape[1]), F32)
        for r0 in range(0, s, rows):
            dst[halo + r0:halo + r0 + rows, :] = src[0, r0:r0 + rows, :].astype(F32)

    r = lax.broadcasted_iota(jnp.int32, (L, L), 0)
    c = lax.broadcasted_iota(jnp.int32, (L, L), 1)
    eye = r == c
    causal = c <= r
    prefix = jnp.where(causal, 1.0, 0.0).astype(BF16)
    glane = lax.broadcasted_iota(jnp.int32, (1, 2 * n_heads), 1)

    def conv_silu(pad_ref, w_ref, t0):
        ext = pad_ref[pl.ds(t0, L + halo), :]
        y = jnp.zeros((L, ext.shape[1]), F32)
        for j in range(conv_width):
            off = halo - (conv_width - 1) + j
            y = y + ext[off:off + L, :] * w_ref[j:j + 1, :]
        return y * _sigmoid(y)

    def to_row(col):
        return jnp.sum(jnp.where(eye, col, 0.0), axis=0, keepdims=True)

    def chunk(ci, carry):
        t0 = pl.multiple_of(ci * L, L)
        g = gate_ref[0, pl.ds(t0, L), :]
        i_all = g + big_ref[...]
        lf_hi, lf_lo = _split_bf16(_log_sigmoid_neg(-(g + bfg_ref[...])))
        a_all = _dot(prefix, lf_hi) + _dot(prefix, lf_lo)
        qc = conv_silu(qpad_ref, cwq_ref, t0)
        kc = conv_silu(kpad_ref, cwk_ref, t0) * (dh ** -0.5)
        new_carry = []
        for hh in range(heads_per_step):
            ct, n_row, m_st = carry[hh]
            col = hp * heads_per_step + hh
            pick_i = glane == col
            pick_f = glane == (n_heads + col)
            i_col = jnp.sum(jnp.where(pick_i, i_all, 0.0), axis=1, keepdims=True)
            a_col = jnp.sum(jnp.where(pick_f, a_all, 0.0), axis=1, keepdims=True)
            a_row = to_row(a_col)
            i_row = to_row(i_col)
            sl = slice(hh * dh, (hh + 1) * dh)
            q_h = qc[:, sl].astype(BF16)
            k_h = kc[:, sl]
            k_hb = k_h.astype(BF16)
            v_h = v_ref[0, pl.ds(t0, L), sl]

            log_d = jnp.where(causal, a_col - a_row + i_row, NEG_INF)
            log_inter = a_col + m_st
            m_row = jnp.maximum(jnp.max(log_d, axis=-1, keepdims=True), log_inter)
            w_intra = jnp.exp(log_d - m_row)
            w_inter = jnp.exp(log_inter - m_row)
            s_qk = _dot(q_h, k_hb, NT_DIMS) * w_intra
            num = _dot(s_qk.astype(BF16), v_h) + w_inter * _dot(q_h, ct.astype(BF16))
            qn = jnp.sum(q_h.astype(F32) * n_row, axis=-1, keepdims=True)
            den = jnp.sum(s_qk, axis=-1, keepdims=True) + w_inter * qn
            h = num / jnp.maximum(jnp.abs(den), jnp.exp(-m_row))

            a_end = a_col[L - 1:L, :]
            log_w = a_end - a_col + i_col
            m_new = jnp.maximum(a_end + m_st, jnp.max(log_w, axis=0, keepdims=True))
            w_s = jnp.exp(log_w - m_new)
            decay = jnp.exp(a_end + m_st - m_new)
            kw = k_h * w_s
            ct_new = decay * ct + _dot(kw.astype(BF16), v_h, TN_DIMS)
            n_new = decay * n_row + jnp.sum(kw, axis=0, keepdims=True)
            new_carry.append((ct_new, n_new, m_new))

            ms = jnp.mean(h * h, axis=-1, keepdims=True)
            og = _sigmoid(o_ref[0, pl.ds(t0, L), sl].astype(F32))
            y = h * lax.rsqrt(ms + EPS) * gout_ref[0, hh:hh + 1, :] * og
            y_ref[0, pl.ds(t0, L), sl] = y.astype(y_ref.dtype)
        return tuple(new_carry)

    init = tuple((jnp.zeros((dh, dh), F32), jnp.zeros((1, dh), F32), jnp.zeros((1, 1), F32))
                 for _ in range(heads_per_step))
    lax.fori_loop(0, s // L, chunk, init)


def _mlstm(p, gates, conv_w, b_i, b_f, g_out, col0, n_heads, dh, heads_per_step):
    bsz, s, _ = p.shape
    w = heads_per_step * dh
    nstep = n_heads // heads_per_step
    width = n_heads * dh
    cb = col0 // w
    wb = width // w
    conv_width = conv_w.shape[0]
    bias = jnp.concatenate([b_i, b_f]).reshape(1, 2 * n_heads)
    lane = jnp.arange(2 * n_heads) < n_heads
    big = jnp.where(lane, bias, 0.0)
    bfg = jnp.where(lane, 0.0, bias)
    kern = functools.partial(_ml_kernel, heads_per_step=heads_per_step, dh=dh, conv_width=conv_width)
    seq = lambda off: pl.BlockSpec((1, s, w), lambda b, h: (b, 0, cb + off + h))
    return pl.pallas_call(
        kern,
        grid=(bsz, nstep),
        in_specs=[seq(0), seq(wb), seq(2 * wb), seq(3 * wb),
                  pl.BlockSpec((1, s, 2 * n_heads), lambda b, h: (b, 0, 0)),
                  pl.BlockSpec((conv_width, w), lambda b, h: (0, h)),
                  pl.BlockSpec((conv_width, w), lambda b, h: (0, wb + h)),
                  pl.BlockSpec((1, 2 * n_heads), lambda b, h: (0, 0)),
                  pl.BlockSpec((1, 2 * n_heads), lambda b, h: (0, 0)),
                  pl.BlockSpec((1, heads_per_step, dh), lambda b, h: (h, 0, 0))],
        out_specs=pl.BlockSpec((1, s, w), lambda b, h: (b, 0, h)),
        out_shape=jax.ShapeDtypeStruct((bsz, s, width), BF16),
        scratch_shapes=[pltpu.VMEM((s + 8, w), F32), pltpu.VMEM((s + 8, w), F32)],
        compiler_params=_params("parallel", "parallel"),
    )(p, p, p, p, gates, conv_w, conv_w, big, bfg,
      g_out.reshape(nstep, heads_per_step, dh))


def _outproj_kernel(ysb_ref, yml_ref, x_ref, mod_ref, g_ref, wo_ref, wq_ref, keys_ref,
                    x1_ref, h2_ref, sc_ref):
    wsb = ysb_ref.shape[2]
    mix = _dot(ysb_ref[0], wo_ref[0:wsb, :]) + _dot(yml_ref[0], wo_ref[wsb:, :])
    x1 = x_ref[0] + mod_ref[0, 2:3, :] * mix
    x1_ref[0] = x1
    h2 = _modulated_norm(x1, g_ref[...], mod_ref[0, 4:5, :], mod_ref[0, 3:4, :]).astype(BF16)
    h2_ref[0] = h2
    half = keys_ref.shape[2]
    for i in range(keys_ref.shape[0]):
        qb = _dot(h2, wq_ref[:, i * half:(i + 1) * half]).astype(BF16)
        sc_ref[i, 0] = _dot(keys_ref[i], qb, NT_DIMS)


def _outproj(ysb, yml, x, mod, g, w_out, w_q, keys, tm):
    bsz, s, d = x.shape
    nsk, nk, half = keys.shape
    return pl.pallas_call(
        _outproj_kernel,
        grid=(bsz, s // tm),
        in_specs=[pl.BlockSpec((1, tm, ysb.shape[2]), lambda b, i: (b, i, 0)),
                  pl.BlockSpec((1, tm, yml.shape[2]), lambda b, i: (b, i, 0)),
                  pl.BlockSpec((1, tm, d), lambda b, i: (b, i, 0)),
                  pl.BlockSpec((1, N_MOD, d), lambda b, i: (b, 0, 0)),
                  pl.BlockSpec((1, d), lambda b, i: (0, 0)),
                  pl.BlockSpec(w_out.shape, lambda b, i: (0, 0)),
                  pl.BlockSpec(w_q.shape, lambda b, i: (0, 0)),
                  pl.BlockSpec(keys.shape, lambda b, i: (0, 0, 0))],
        out_specs=[pl.BlockSpec((1, tm, d), lambda b, i: (b, i, 0)),
                   pl.BlockSpec((1, tm, d), lambda b, i: (b, i, 0)),
                   pl.BlockSpec((nsk, 1, nk, tm), lambda b, i: (0, b, 0, i))],
        out_shape=[jax.ShapeDtypeStruct((bsz, s, d), F32),
                   jax.ShapeDtypeStruct((bsz, s, d), BF16),
                   jax.ShapeDtypeStruct((nsk, bsz, nk, s), F32)],
        compiler_params=_params("parallel", "parallel"),
    )(ysb, yml, x, mod, g, w_out, w_q, keys)


def _extract_top(x, k, pos):
    n = x.shape[0]
    rank = jnp.full(x.shape, float(k), F32)
    tops = []
    for a in range(k):
        m = jnp.max(x, axis=0, keepdims=True)
        first = jnp.min(jnp.where(x == m, pos, float(n)), axis=0, keepdims=True)
        pick = pos == first
        rank = jnp.where(pick, float(a), rank)
        x = jnp.where(pick, NEG_INF, x)
        tops.append(m)
    return tops, rank


def _extract_top_fast(x, k):
    rank = jnp.full(x.shape, float(k), F32)
    tops = []
    for a in range(k):
        m = jnp.max(x, axis=0, keepdims=True)
        pick = x == m
        rank = jnp.where(pick, float(a), rank)
        x = jnp.where(pick, NEG_INF, x)
        tops.append(m)
    return tops, rank


SUBLANES = 8
COUNT_BITS = 5
COUNT_MASK = (1 << COUNT_BITS) - 1
ROUTE_ROWS = PEER_TOPK + SUBLANES * (PEER_TOPK // 2 - 1) + PEER_TOPK // 2


def _route_kernel(sc_ref, cc_ref, rank_ref, e1_ref, top_ref, cand_ref, sel_ref, rk_ref):
    n_heads = cc_ref.shape[0]
    nk, tb = sc_ref.shape[2], sc_ref.shape[3]
    K = PEER_TOPK
    assert K == 2 * SUBLANES
    pos = lax.broadcasted_iota(jnp.int32, (nk, tb), 0).astype(F32)
    cpos = lax.broadcasted_iota(jnp.int32, (ROUTE_ROWS, tb), 0).astype(F32)
    sub = lax.broadcasted_iota(jnp.int32, (SUBLANES, tb), 0)
    group = lambda a: slice(K + SUBLANES * (a - 1), K + SUBLANES * a)
    tail = slice(K + SUBLANES * (K // 2 - 1), ROUTE_ROWS)

    def tied(rank):
        n_ranked = jnp.sum(jnp.where(rank < float(K), 1.0, 0.0), axis=0, keepdims=True)
        return jnp.max(n_ranked) > float(K)

    def head(h, carry):
        s0 = sc_ref[2 * h, 0]
        s1 = sc_ref[2 * h + 1, 0]
        def keep(p, tops, rank):
            for a in range(K):
                top_ref[p, a:a + 1, :] = tops[a]
            rk_ref[p] = rank

        fast = [_extract_top_fast(x, K) for x in (s0, s1)]
        for p in range(2):
            keep(p, *fast[p])
        retry = [tied(fast[p][1]) for p in range(2)]
        for p, x in ((0, s0), (1, s1)):
            pl.when(retry[p])(lambda p=p, x=x: keep(p, *_extract_top(x, K, pos)))
        rank0, rank1 = rk_ref[0], rk_ref[1]
        top0, top1 = top_ref[0, 0:1, :], top_ref[1, 0:1, :]

        cand_ref[0:K, :] = top0 + top_ref[1]
        head1 = top_ref[1, 0:SUBLANES, :]
        for a in range(1, K // 2):
            cand_ref[group(a), :] = jnp.where(sub < K // (a + 1), top_ref[0, a:a + 1, :] + head1, NEG_INF)
        cand_ref[tail, :] = top_ref[0, K // 2:K, :] + top1
        cand = cand_ref[...]

        def keep_sel(pick_round):
            sel_ref[...] = jnp.where(pick_round < float(K), 1.0, 0.0)

        _, pick_round = _extract_top_fast(cand, K)
        keep_sel(pick_round)
        pl.when(tied(pick_round))(lambda: keep_sel(_extract_top(cand_ref[...], K, cpos)[1]))
        z = jnp.sum(sel_ref[...] * jnp.exp(cand - (top0 + top1)), axis=0, keepdims=True)

        cnt = jnp.where(rank0 == 0.0, jnp.sum(sel_ref[0:K, :], axis=0, keepdims=True), 0.0)
        for a in range(1, K // 2):
            cnt = jnp.where(rank0 == float(a), jnp.sum(sel_ref[group(a), :], axis=0, keepdims=True), cnt)
        for a in range(K // 2, K):
            row = tail.start + a - K // 2
            cnt = jnp.where(rank0 == float(a), sel_ref[row:row + 1, :], cnt)
        c0_bits = pltpu.bitcast(jnp.exp(s0 - top0) / z, jnp.int32)
        cc_ref[h, 0] = pltpu.bitcast((c0_bits & ~COUNT_MASK) | cnt.astype(jnp.int32), F32)
        rank_ref[h, 0] = pltpu.bitcast(rank1.astype(BF16), jnp.uint32)
        e1_ref[h, 0] = pltpu.bitcast(jnp.exp(s1 - top1).astype(BF16), jnp.uint32)
        return carry

    lax.fori_loop(0, n_heads, head, 0)


def _route(scores, tb):
    nsk, bsz, nk, s = scores.shape
    n_heads = nsk // 2
    spec = lambda rows: pl.BlockSpec((n_heads, 1, rows, tb), lambda b, i: (0, b, 0, i))
    words = jax.ShapeDtypeStruct((n_heads, bsz, nk // 2, s), jnp.uint32)
    return pl.pallas_call(
        _route_kernel,
        grid=(bsz, s // tb),
        in_specs=[pl.BlockSpec((nsk, 1, nk, tb), lambda b, i: (0, b, 0, i))],
        out_specs=[spec(nk), spec(nk // 2), spec(nk // 2)],
        out_shape=[jax.ShapeDtypeStruct((n_heads, bsz, nk, s), F32), words, words],
        scratch_shapes=[pltpu.VMEM((2, PEER_TOPK, tb), F32),
                        pltpu.VMEM((ROUTE_ROWS, tb), F32),
                        pltpu.VMEM((ROUTE_ROWS, tb), F32),
                        pltpu.VMEM((2, nk, tb), F32)],
        compiler_params=_params("parallel", "parallel"),
    )(scores)


def _gelu(x):
    return 0.5 * x * (1.0 + lax.erf(x * math.sqrt(0.5)))


def _peer_kernel(pace_ref, h2_ref, u_ref, vt_ref, cc_ref, rank_ref, e1_ref, o_ref, act_ref, coef_ref,
                 *, groups, slice_lanes):
    j = pl.program_id(2)
    n_heads, _, nk, tb = cc_ref.shape
    te = groups * nk
    rows = BF16_SUBLANES
    words = rows // 2
    n_slices = tb // slice_lanes
    n_sub = 2 * n_slices
    part = lambda k: (k // n_slices, slice((k % n_slices) * slice_lanes, (k % n_slices + 1) * slice_lanes))

    def activations(k):
        slot, tok = part(k)
        act_ref[slot, :, tok] = _dot(u_ref[slot * te:(slot + 1) * te, :], h2_ref[0, tok, :], NT_DIMS)

    def apply(k):
        slot, tok = part(k)
        o_ref[0, :, tok] += _dot(vt_ref[slot], coef_ref[slot, :, tok])

    def coefficients(k):
        slot, tok = part(k)
        for gg in range(groups):
            g = (2 * j + slot) * groups + gg
            cc_rows = [cc_ref[h, 0, pl.ds(g, 1), :] for h in range(n_heads)]
            for l0 in range(tok.start, tok.stop, LANES):
                ln = slice(l0, l0 + LANES)
                n0, c0 = [], []
                for row in cc_rows:
                    tile = jnp.broadcast_to(row[:, ln], (rows, LANES))
                    count = pltpu.bitcast(tile, jnp.int32) & COUNT_MASK
                    n0.append(count.astype(F32).astype(BF16))
                    c0.append(tile.astype(BF16))
                for r0 in range(0, nk, rows):
                    wr = slice(r0 // 2, r0 // 2 + words)
                    cf = jnp.zeros((rows, LANES), BF16)
                    for h in range(n_heads):
                        rank = pltpu.bitcast(rank_ref[h, 0, wr, ln], BF16)
                        e1 = pltpu.bitcast(e1_ref[h, 0, wr, ln], BF16)
                        cf = cf + jnp.where(rank < n0[h], e1 * c0[h], 0.0)
                    er = slice(gg * nk + r0, gg * nk + r0 + rows)
                    coef_ref[slot, er, ln] = cf * _gelu(act_ref[slot, er, ln]).astype(BF16)

    @pl.when(j == 0)
    def _():
        o_ref[...] = jnp.zeros_like(o_ref)

    paced = pace_ref[0] > 0
    pl.when(paced)(lambda: activations(0))
    for k in range(n_sub):
        @pl.when(paced)
        def _(k=k):
            if k + 1 < n_sub:
                activations(k + 1)
            if k > 0:
                apply(k - 1)
            coefficients(k)
    pl.when(paced)(lambda: apply(n_sub - 1))


def _peer(h2, u, vt3, cc, rank, e1, tb, groups):
    bsz, s, d = h2.shape
    n_heads, _, nk, _ = cc.shape
    te = groups * nk
    wspec = pl.BlockSpec((n_heads, 1, nk // 2, tb), lambda b, i, j: (0, b, 0, i))
    return pl.pallas_call(
        functools.partial(_peer_kernel, groups=groups, slice_lanes=min(2 * LANES, tb)),
        grid=(bsz, s // tb, u.shape[0] // (2 * te)),
        in_specs=[pl.BlockSpec(memory_space=pltpu.SMEM),
                  pl.BlockSpec((1, tb, d), lambda b, i, j: (b, i, 0)),
                  pl.BlockSpec((2 * te, d), lambda b, i, j: (j, 0)),
                  pl.BlockSpec((2, d, te), lambda b, i, j: (j, 0, 0)),
                  pl.BlockSpec((n_heads, 1, nk, tb), lambda b, i, j: (0, b, 0, i)),
                  wspec, wspec],
        out_specs=pl.BlockSpec((1, d, tb), lambda b, i, j: (b, 0, i)),
        out_shape=jax.ShapeDtypeStruct((bsz, d, s), F32),
        scratch_shapes=[pltpu.VMEM((2, te, tb), F32), pltpu.VMEM((2, te, tb), BF16)],
        compiler_params=_params("parallel", "parallel", "arbitrary"),
    )(jnp.ones((1,), jnp.int32), h2, u, vt3, cc, rank, e1)


def _residual_kernel(x1_ref, yt_ref, mod_ref, o_ref):
    o_ref[0] = x1_ref[0] + mod_ref[0, 5:6, :] * yt_ref[0].T


def _residual(x1, yt, mod, tm):
    bsz, s, d = x1.shape
    return pl.pallas_call(
        _residual_kernel,
        grid=(bsz, s // tm),
        in_specs=[pl.BlockSpec((1, tm, d), lambda b, i: (b, i, 0)),
                  pl.BlockSpec((1, d, tm), lambda b, i: (b, 0, i)),
                  pl.BlockSpec((1, N_MOD, d), lambda b, i: (b, 0, 0))],
        out_specs=pl.BlockSpec((1, tm, d), lambda b, i: (b, i, 0)),
        out_shape=jax.ShapeDtypeStruct((bsz, s, d), F32),
        compiler_params=_params("parallel", "parallel"),
    )(x1, yt, mod)


def _layer(x, c, w_ada, b_ada, g_norm1, w_in, b_igate, b_fgate, conv_w, g_q_sb, g_k_sb,
           g_out_sb, g_out_ml, w_out, g_norm2, w_q_peer, sub_keys, expert_u, expert_v):
    bsz, s, d = x.shape
    sb_heads, sb_hd = g_out_sb.shape
    ml_heads, ml_hd = g_out_ml.shape
    sb_width = sb_heads * sb_hd
    ml_width = ml_heads * ml_hd
    n_main = 3 * sb_width + 4 * ml_width
    peer_heads, _, n_keys, half = sub_keys.shape

    tm = min(512, s)
    tq = min(256, s)
    tb = min(1024, s)

    mod = _ada(c, w_ada, b_ada).reshape(bsz, N_MOD, d)
    p, gates = _inproj(x, mod, g_norm1.reshape(1, d), w_in[:, :n_main].astype(BF16),
                       w_in[:, n_main:].astype(BF16), tm)

    tile2 = lambda g: jnp.concatenate([g, g], axis=-1)
    ysb = _sb_attention(p, tile2(g_q_sb).reshape(1, 2 * sb_hd), tile2(g_k_sb).reshape(1, 2 * sb_hd),
                        g_out_sb.reshape(sb_heads // 2, 1, 2 * sb_hd), sb_heads, sb_hd, tq)
    yml = _mlstm(p, gates, conv_w, b_igate, b_fgate, g_out_ml, 3 * sb_width, ml_heads, ml_hd,
                 heads_per_step=2)

    x1, h2, scores = _outproj(ysb, yml, x, mod, g_norm2.reshape(1, d), w_out.astype(BF16),
                              w_q_peer.astype(BF16),
                              sub_keys.reshape(2 * peer_heads, n_keys, half).astype(BF16),
                              min(256, s))
    cc, rank, e1 = _route(scores, min(256, s))
    groups = 4
    te = groups * n_keys
    vt3 = expert_v.astype(BF16).reshape(-1, te, d).transpose(0, 2, 1)
    yt = _peer(h2, expert_u.astype(BF16), vt3, cc, rank, e1, tb, groups)
    return _residual(x1, yt, mod, tm)


def kernel(x, c, w_ada, b_ada, g_norm1, w_in, b_igate, b_fgate, conv_w, g_q_sb, g_k_sb, g_out_sb,
           g_out_ml, w_out, g_norm2, w_q_peer, sub_keys, expert_u, expert_v):
    params = (w_ada, b_ada, g_norm1, w_in, b_igate, b_fgate, conv_w, g_q_sb, g_k_sb, g_out_sb,
              g_out_ml, w_out, g_norm2, w_q_peer, sub_keys, expert_u, expert_v)
    for layer in range(w_ada.shape[0]):
        x = _layer(x, c, *(t[layer] for t in params))
    return x
```

```python
import functools
import math

import jax
import jax.numpy as jnp
from jax import lax
from jax.experimental import pallas as pl
from jax.experimental.pallas import tpu as pltpu

F32 = jnp.float32
BF16 = jnp.bfloat16

EPS = 1e-6
N_MOD = 6
MLSTM_CHUNK = 64
PEER_TOPK = 16
NEG_INF = float("-inf")
F32_EXP_UNDERFLOW = math.log(2.0 ** -126)

V7X_VMEM_BYTES = 64 * 1024 * 1024
VMEM_LIMIT = V7X_VMEM_BYTES - 12 * 1024 * 1024
LANES = 128
BF16_SUBLANES = 16

NT_DIMS = (((1,), (1,)), ((), ()))
NN_DIMS = (((1,), (0,)), ((), ()))
TN_DIMS = (((0,), (0,)), ((), ()))


def _params(*sem):
    return pltpu.CompilerParams(dimension_semantics=sem, vmem_limit_bytes=VMEM_LIMIT)


def _dot(a, b, dims=NN_DIMS):
    return lax.dot_general(a, b, dims, preferred_element_type=F32)


def _split_bf16(a):
    hi = a.astype(BF16)
    lo = (a - hi.astype(F32)).astype(BF16)
    return hi, lo


def _dot_f32(a, b, dims=NN_DIMS):
    ah, al = _split_bf16(a)
    bh, bl = _split_bf16(b)
    return _dot(ah, bh, dims) + (_dot(ah, bl, dims) + _dot(al, bh, dims))


def _dot_exact_rhs(a, b_bf16, dims=NN_DIMS):
    ah, al = _split_bf16(a)
    return _dot(ah, b_bf16, dims) + _dot(al, b_bf16, dims)


def _sigmoid(x):
    return 1.0 / (1.0 + jnp.exp(-x))


def _log_sigmoid_neg(z):
    return -(jnp.maximum(z, 0.0) + jnp.log(1.0 + jnp.exp(-jnp.abs(z))))


def _ada_kernel(c_ref, w_ref, b_ref, o_ref):
    c = c_ref[...]
    o_ref[...] = _dot_f32(c * _sigmoid(c), w_ref[...]) + b_ref[...]


def _ada(c, w, b):
    bsz, d = c.shape
    n = w.shape[1]
    tn = n // N_MOD
    return pl.pallas_call(
        _ada_kernel,
        grid=(n // tn,),
        in_specs=[pl.BlockSpec((bsz, d), lambda j: (0, 0)),
                  pl.BlockSpec((d, tn), lambda j: (0, j)),
                  pl.BlockSpec((1, tn), lambda j: (0, j))],
        out_specs=pl.BlockSpec((bsz, tn), lambda j: (0, j)),
        out_shape=jax.ShapeDtypeStruct((bsz, n), F32),
        compiler_params=_params("parallel"),
    )(c, w, b.reshape(1, n))


def _modulated_norm(x, g, scale, shift):
    y = x * lax.rsqrt(jnp.mean(x * x, axis=-1, keepdims=True) + EPS)
    return (y * g) * (1.0 + scale) + shift


def _inproj_kernel(x_ref, mod_ref, g_ref, w_ref, wg_ref, p_ref, gate_ref, *, n_chunk):
    h = _modulated_norm(x_ref[0], g_ref[...], mod_ref[0, 1:2, :], mod_ref[0, 0:1, :])
    hb = h.astype(BF16)
    for n0 in range(0, w_ref.shape[1], n_chunk):
        p_ref[0, :, n0:n0 + n_chunk] = _dot(hb, w_ref[:, n0:n0 + n_chunk]).astype(BF16)
    gate_ref[0] = _dot(hb, wg_ref[...])


def _inproj(x, mod, g, w_main, w_gate, tm):
    bsz, s, d = x.shape
    n = w_main.shape[1]
    ng = w_gate.shape[1]
    return pl.pallas_call(
        functools.partial(_inproj_kernel, n_chunk=512),
        grid=(bsz, s // tm),
        in_specs=[pl.BlockSpec((1, tm, d), lambda b, i: (b, i, 0)),
                  pl.BlockSpec((1, N_MOD, d), lambda b, i: (b, 0, 0)),
                  pl.BlockSpec((1, d), lambda b, i: (0, 0)),
                  pl.BlockSpec((d, n), lambda b, i: (0, 0)),
                  pl.BlockSpec((d, ng), lambda b, i: (0, 0))],
        out_specs=[pl.BlockSpec((1, tm, n), lambda b, i: (b, i, 0)),
                   pl.BlockSpec((1, tm, ng), lambda b, i: (b, i, 0))],
        out_shape=[jax.ShapeDtypeStruct((bsz, s, n), BF16),
                   jax.ShapeDtypeStruct((bsz, s, ng), F32)],
        compiler_params=_params("parallel", "parallel"),
    )(x, mod, g, w_main, w_gate)


def _pair_rms(x, g2, hd):
    lane = lax.broadcasted_iota(jnp.int32, x.shape, 1)
    sq = x * x
    s0 = jnp.sum(jnp.where(lane < hd, sq, 0.0), axis=-1, keepdims=True)
    s1 = jnp.sum(jnp.where(lane < hd, 0.0, sq), axis=-1, keepdims=True)
    inv = jnp.where(lane < hd, lax.rsqrt(s0 / hd + EPS), lax.rsqrt(s1 / hd + EPS))
    return x * inv * g2


def _sb_kernel(q_ref, k_ref, v_ref, gq_ref, gk_ref, go_ref, o_ref, kn_ref, acc_ref, run_ref,
               *, tq, hd):
    qi = pl.program_id(2)
    s = k_ref.shape[1]
    rows = 512 if s % 512 == 0 else tq

    @pl.when(qi == 0)
    def _():
        for r0 in range(0, s, rows):
            kk = k_ref[0, r0:r0 + rows, :].astype(F32)
            kn_ref[r0:r0 + rows, :] = _pair_rms(kk, gk_ref[...], hd).astype(BF16)

    qn = _pair_rms(q_ref[0].astype(F32), gq_ref[...], hd) * (hd ** -0.5)
    lane = lax.broadcasted_iota(jnp.int32, qn.shape, 1)
    q2 = jnp.concatenate([jnp.where(lane < hd, qn, 0.0), jnp.where(lane < hd, 0.0, qn)],
                         axis=0).astype(BF16)

    r = lax.broadcasted_iota(jnp.int32, (tq, tq), 0)
    c = lax.broadcasted_iota(jnp.int32, (tq, tq), 1)
    suffix = jnp.where(r >= c, 1.0, 0.0).astype(BF16)
    r2 = lax.broadcasted_iota(jnp.int32, (2 * tq, tq), 0)
    c2 = lax.broadcasted_iota(jnp.int32, (2 * tq, tq), 1)
    visible = c2 < jnp.where(r2 < tq, r2, r2 - tq)

    def block(j, diagonal):
        k0 = pl.multiple_of(j * tq, tq)
        z = _dot(q2, kn_ref[pl.ds(k0, tq), :], NT_DIMS)
        lk = _log_sigmoid_neg(z)
        if diagonal:
            lk = jnp.where(visible, lk, 0.0)
        cs = _dot_exact_rhs(lk, suffix)
        run = run_ref[...]
        w = jnp.exp(z + cs + run)
        if diagonal:
            w = jnp.where(visible, w, 0.0)
        acc_ref[...] += _dot(w.astype(BF16), v_ref[0, pl.ds(k0, tq), :])
        run = run + cs[:, 0:1]
        run_ref[...] = run
        return jnp.max(run)

    acc_ref[...] = jnp.zeros_like(acc_ref)
    run_ref[...] = jnp.zeros_like(run_ref)
    z_bound = 1.02 * (hd ** 0.5) * jnp.max(jnp.abs(gq_ref[...])) * jnp.max(jnp.abs(gk_ref[...]))

    def more(state):
        j, top = state
        return jnp.logical_and(j >= 0, top + z_bound > F32_EXP_UNDERFLOW)

    lax.while_loop(more, lambda st: (st[0] - 1, block(st[0], False)), (qi - 1, block(qi, True)))

    y = jnp.where(lane < hd, acc_ref[0:tq, :], acc_ref[tq:2 * tq, :])
    o_ref[0] = _pair_rms(y, go_ref[0], hd).astype(o_ref.dtype)


def _sb_attention(p, g_q2, g_k2, g_out2, n_heads, hd, tq):
    bsz, s, _ = p.shape
    w = 2 * hd
    npair = n_heads // 2
    return pl.pallas_call(
        functools.partial(_sb_kernel, tq=tq, hd=hd),
        grid=(bsz, npair, s // tq),
        in_specs=[pl.BlockSpec((1, tq, w), lambda b, h, i: (b, i, h)),
                  pl.BlockSpec((1, s, w), lambda b, h, i: (b, 0, npair + h)),
                  pl.BlockSpec((1, s, w), lambda b, h, i: (b, 0, 2 * npair + h)),
                  pl.BlockSpec((1, w), lambda b, h, i: (0, 0)),
                  pl.BlockSpec((1, w), lambda b, h, i: (0, 0)),
                  pl.BlockSpec((1, 1, w), lambda b, h, i: (h, 0, 0))],
        out_specs=pl.BlockSpec((1, tq, w), lambda b, h, i: (b, i, h)),
        out_shape=jax.ShapeDtypeStruct((bsz, s, n_heads * hd), BF16),
        scratch_shapes=[pltpu.VMEM((s, w), BF16), pltpu.VMEM((2 * tq, w), F32),
                        pltpu.VMEM((2 * tq, 1), F32)],
        compiler_params=_params("parallel", "parallel", "arbitrary"),
    )(p, p, p, g_q2, g_k2, g_out2)


def _ml_kernel(q_ref, k_ref, v_ref, o_ref, gate_ref, cwq_ref, cwk_ref, big_ref, bfg_ref,
               gout_ref, y_ref, qpad_ref, kpad_ref, *, heads_per_step, dh, conv_width):
    L = MLSTM_CHUNK
    s = q_ref.shape[1]
    n_heads = gate_ref.shape[2] // 2
    hp = pl.program_id(1)
    halo = 8
    rows = 512 if s % 512 == 0 else L

    for src, dst in ((q_ref, qpad_ref), (k_ref, kpad_ref)):
        dst[0:halo, :] = jnp.zeros((halo, dst.shape[1]), F32)
        for r0 in range(0, s, rows):
            dst[halo + r0:halo + r0 + rows, :] = src[0, r0:r0 + rows, :].astype(F32)

    r = lax.broadcasted_iota(jnp.int32, (L, L), 0)
    c = lax.broadcasted_iota(jnp.int32, (L, L), 1)
    eye = r == c
    causal = c <= r
    prefix = jnp.where(causal, 1.0, 0.0).astype(BF16)
    glane = lax.broadcasted_iota(jnp.int32, (1, 2 * n_heads), 1)

    def conv_silu(pad_ref, w_ref, t0):
        ext = pad_ref[pl.ds(t0, L + halo), :]
        y = jnp.zeros((L, ext.shape[1]), F32)
        for j in range(conv_width):
            off = halo - (conv_width - 1) + j
            y = y + ext[off:off + L, :] * w_ref[j:j + 1, :]
        return y * _sigmoid(y)

    def to_row(col):
        return jnp.sum(jnp.where(eye, col, 0.0), axis=0, keepdims=True)

    def chunk(ci, carry):
        t0 = pl.multiple_of(ci * L, L)
        g = gate_ref[0, pl.ds(t0, L), :]
        i_all = g + big_ref[...]
        lf_hi, lf_lo = _split_bf16(_log_sigmoid_neg(-(g + bfg_ref[...])))
        a_all = _dot(prefix, lf_hi) + _dot(prefix, lf_lo)
        qc = conv_silu(qpad_ref, cwq_ref, t0)
        kc = conv_silu(kpad_ref, cwk_ref, t0) * (dh ** -0.5)
        new_carry = []
        for hh in range(heads_per_step):
            ct, n_row, m_st = carry[hh]
            col = hp * heads_per_step + hh
            pick_i = glane == col
            pick_f = glane == (n_heads + col)
            i_col = jnp.sum(jnp.where(pick_i, i_all, 0.0), axis=1, keepdims=True)
            a_col = jnp.sum(jnp.where(pick_f, a_all, 0.0), axis=1, keepdims=True)
            a_row = to_row(a_col)
            i_row = to_row(i_col)
            sl = slice(hh * dh, (hh + 1) * dh)
            q_h = qc[:, sl].astype(BF16)
            k_h = kc[:, sl]
            k_hb = k_h.astype(BF16)
            v_h = v_ref[0, pl.ds(t0, L), sl]

            log_d = jnp.where(causal, a_col - a_row + i_row, NEG_INF)
            log_inter = a_col + m_st
            m_row = jnp.maximum(jnp.max(log_d, axis=-1, keepdims=True), log_inter)
            w_intra = jnp.exp(log_d - m_row)
            w_inter = jnp.exp(log_inter - m_row)
            s_qk = _dot(q_h, k_hb, NT_DIMS) * w_intra
            num = _dot(s_qk.astype(BF16), v_h) + w_inter * _dot(q_h, ct.astype(BF16))
            qn = jnp.sum(q_h.astype(F32) * n_row, axis=-1, keepdims=True)
            den = jnp.sum(s_qk, axis=-1, keepdims=True) + w_inter * qn
            h = num / jnp.maximum(jnp.abs(den), jnp.exp(-m_row))

            a_end = a_col[L - 1:L, :]
            log_w = a_end - a_col + i_col
            m_new = jnp.maximum(a_end + m_st, jnp.max(log_w, axis=0, keepdims=True))
            w_s = jnp.exp(log_w - m_new)
            decay = jnp.exp(a_end + m_st - m_new)
            kw = k_h * w_s
            ct_new = decay * ct + _dot(kw.astype(BF16), v_h, TN_DIMS)
            n_new = decay * n_row + jnp.sum(kw, axis=0, keepdims=True)
            new_carry.append((ct_new, n_new, m_new))

            ms = jnp.mean(h * h, axis=-1, keepdims=True)
            og = _sigmoid(o_ref[0, pl.ds(t0, L), sl].astype(F32))
            y = h * lax.rsqrt(ms + EPS) * gout_ref[0, hh:hh + 1, :] * og
            y_ref[0, pl.ds(t0, L), sl] = y.astype(y_ref.dtype)
        return tuple(new_carry)

    init = tuple((jnp.zeros((dh, dh), F32), jnp.zeros((1, dh), F32), jnp.zeros((1, 1), F32))
                 for _ in range(heads_per_step))
    lax.fori_loop(0, s // L, chunk, init)


def _mlstm(p, gates, conv_w, b_i, b_f, g_out, col0, n_heads, dh, heads_per_step):
    bsz, s, _ = p.shape
    w = heads_per_step * dh
    nstep = n_heads // heads_per_step
    width = n_heads * dh
    cb = col0 // w
    wb = width // w
    conv_width = conv_w.shape[0]
    bias = jnp.concatenate([b_i, b_f]).reshape(1, 2 * n_heads)
    lane = jnp.arange(2 * n_heads) < n_heads
    big = jnp.where(lane, bias, 0.0)
    bfg = jnp.where(lane, 0.0, bias)
    kern = functools.partial(_ml_kernel, heads_per_step=heads_per_step, dh=dh, conv_width=conv_width)
    seq = lambda off: pl.BlockSpec((1, s, w), lambda b, h: (b, 0, cb + off + h))
    return pl.pallas_call(
        kern,
        grid=(bsz, nstep),
        in_specs=[seq(0), seq(wb), seq(2 * wb), seq(3 * wb),
                  pl.BlockSpec((1, s, 2 * n_heads), lambda b, h: (b, 0, 0)),
                  pl.BlockSpec((conv_width, w), lambda b, h: (0, h)),
                  pl.BlockSpec((conv_width, w), lambda b, h: (0, wb + h)),
                  pl.BlockSpec((1, 2 * n_heads), lambda b, h: (0, 0)),
                  pl.BlockSpec((1, 2 * n_heads), lambda b, h: (0, 0)),
                  pl.BlockSpec((1, heads_per_step, dh), lambda b, h: (h, 0, 0))],
        out_specs=pl.BlockSpec((1, s, w), lambda b, h: (b, 0, h)),
        out_shape=jax.ShapeDtypeStruct((bsz, s, width), BF16),
        scratch_shapes=[pltpu.VMEM((s + 8, w), F32), pltpu.VMEM((s + 8, w), F32)],
        compiler_params=_params("parallel", "parallel"),
    )(p, p, p, p, gates, conv_w, conv_w, big, bfg,
      g_out.reshape(nstep, heads_per_step, dh))


def _fold_kernel(keys_ref, wq_ref, o_ref):
    o_ref[...] = _dot_f32(keys_ref[0], wq_ref[...], NT_DIMS).astype(BF16)


def _fold_keys(keys, w_q):
    nsk, nk, half = keys.shape
    d = w_q.shape[0]
    return pl.pallas_call(
        _fold_kernel,
        grid=(nsk,),
        in_specs=[pl.BlockSpec((1, nk, half), lambda i: (i, 0, 0)),
                  pl.BlockSpec((d, half), lambda i: (0, i))],
        out_specs=pl.BlockSpec((nk, d), lambda i: (i, 0)),
        out_shape=jax.ShapeDtypeStruct((nsk * nk, d), BF16),
        compiler_params=_params("parallel"),
    )(keys, w_q)


def _outproj_kernel(ysb_ref, yml_ref, x_ref, mod_ref, g_ref, wo_ref, wk_ref, x1_ref, h2_ref, sc_ref):
    wsb = ysb_ref.shape[2]
    mix = _dot(ysb_ref[0], wo_ref[0:wsb, :]) + _dot(yml_ref[0], wo_ref[wsb:, :])
    x1 = x_ref[0] + mod_ref[0, 2:3, :] * mix
    x1_ref[0] = x1
    h2 = _modulated_norm(x1, g_ref[...], mod_ref[0, 4:5, :], mod_ref[0, 3:4, :]).astype(BF16)
    h2_ref[0] = h2
    nk = sc_ref.shape[2]
    scores = _dot(wk_ref[...], h2, NT_DIMS)
    for i in range(sc_ref.shape[0]):
        sc_ref[i, 0] = scores[i * nk:(i + 1) * nk, :]


def _outproj(ysb, yml, x, mod, g, w_out, w_keys, nk, tm):
    bsz, s, d = x.shape
    nsk = w_keys.shape[0] // nk
    return pl.pallas_call(
        _outproj_kernel,
        grid=(bsz, s // tm),
        in_specs=[pl.BlockSpec((1, tm, ysb.shape[2]), lambda b, i: (b, i, 0)),
                  pl.BlockSpec((1, tm, yml.shape[2]), lambda b, i: (b, i, 0)),
                  pl.BlockSpec((1, tm, d), lambda b, i: (b, i, 0)),
                  pl.BlockSpec((1, N_MOD, d), lambda b, i: (b, 0, 0)),
                  pl.BlockSpec((1, d), lambda b, i: (0, 0)),
                  pl.BlockSpec(w_out.shape, lambda b, i: (0, 0)),
                  pl.BlockSpec(w_keys.shape, lambda b, i: (0, 0))],
        out_specs=[pl.BlockSpec((1, tm, d), lambda b, i: (b, i, 0)),
                   pl.BlockSpec((1, tm, d), lambda b, i: (b, i, 0)),
                   pl.BlockSpec((nsk, 1, nk, tm), lambda b, i: (0, b, 0, i))],
        out_shape=[jax.ShapeDtypeStruct((bsz, s, d), F32),
                   jax.ShapeDtypeStruct((bsz, s, d), BF16),
                   jax.ShapeDtypeStruct((nsk, bsz, nk, s), F32)],
        compiler_params=_params("parallel", "parallel"),
    )(ysb, yml, x, mod, g, w_out, w_keys)


def _extract_top(x, k, pos):
    n = x.shape[0]
    rank = jnp.full(x.shape, float(k), F32)
    tops = []
    for a in range(k):
        m = jnp.max(x, axis=0, keepdims=True)
        first = jnp.min(jnp.where(x == m, pos, float(n)), axis=0, keepdims=True)
        pick = pos == first
        rank = jnp.where(pick, float(a), rank)
        x = jnp.where(pick, NEG_INF, x)
        tops.append(m)
    return tops, rank


def _extract_top_fast(x, k):
    rank = jnp.full(x.shape, float(k), F32)
    tops = []
    for a in range(k):
        m = jnp.max(x, axis=0, keepdims=True)
        pick = x == m
        rank = jnp.where(pick, float(a), rank)
        x = jnp.where(pick, NEG_INF, x)
        tops.append(m)
    return tops, rank


SUBLANES = 8
COUNT_BITS = 5
COUNT_MASK = (1 << COUNT_BITS) - 1
ROUTE_ROWS = PEER_TOPK + SUBLANES * (PEER_TOPK // 2 - 1) + PEER_TOPK // 2


def _route_kernel(sc_ref, cc_ref, rank_ref, e1_ref, top_ref, cand_ref, sel_ref, rk_ref):
    n_heads = cc_ref.shape[0]
    nk, tb = sc_ref.shape[2], sc_ref.shape[3]
    K = PEER_TOPK
    assert K == 2 * SUBLANES
    pos = lax.broadcasted_iota(jnp.int32, (nk, tb), 0).astype(F32)
    cpos = lax.broadcasted_iota(jnp.int32, (ROUTE_ROWS, tb), 0).astype(F32)
    sub = lax.broadcasted_iota(jnp.int32, (SUBLANES, tb), 0)
    group = lambda a: slice(K + SUBLANES * (a - 1), K + SUBLANES * a)
    tail = slice(K + SUBLANES * (K // 2 - 1), ROUTE_ROWS)

    def tied(rank):
        n_ranked = jnp.sum(jnp.where(rank < float(K), 1.0, 0.0), axis=0, keepdims=True)
        return jnp.max(n_ranked) > float(K)

    def head(h, carry):
        s0 = sc_ref[2 * h, 0]
        s1 = sc_ref[2 * h + 1, 0]
        def keep(p, tops, rank):
            for a in range(K):
                top_ref[p, a:a + 1, :] = tops[a]
            rk_ref[p] = rank

        fast = [_extract_top_fast(x, K) for x in (s0, s1)]
        for p in range(2):
            keep(p, *fast[p])
        retry = [tied(fast[p][1]) for p in range(2)]
        for p, x in ((0, s0), (1, s1)):
            pl.when(retry[p])(lambda p=p, x=x: keep(p, *_extract_top(x, K, pos)))
        rank0, rank1 = rk_ref[0], rk_ref[1]
        top0, top1 = top_ref[0, 0:1, :], top_ref[1, 0:1, :]

        cand_ref[0:K, :] = top0 + top_ref[1]
        head1 = top_ref[1, 0:SUBLANES, :]
        for a in range(1, K // 2):
            cand_ref[group(a), :] = jnp.where(sub < K // (a + 1), top_ref[0, a:a + 1, :] + head1, NEG_INF)
        cand_ref[tail, :] = top_ref[0, K // 2:K, :] + top1
        cand = cand_ref[...]

        def keep_sel(pick_round):
            sel_ref[...] = jnp.where(pick_round < float(K), 1.0, 0.0)

        _, pick_round = _extract_top_fast(cand, K)
        keep_sel(pick_round)
        pl.when(tied(pick_round))(lambda: keep_sel(_extract_top(cand_ref[...], K, cpos)[1]))
        z = jnp.sum(sel_ref[...] * jnp.exp(cand - (top0 + top1)), axis=0, keepdims=True)

        cnt = jnp.where(rank0 == 0.0, jnp.sum(sel_ref[0:K, :], axis=0, keepdims=True), 0.0)
        for a in range(1, K // 2):
            cnt = jnp.where(rank0 == float(a), jnp.sum(sel_ref[group(a), :], axis=0, keepdims=True), cnt)
        for a in range(K // 2, K):
            row = tail.start + a - K // 2
            cnt = jnp.where(rank0 == float(a), sel_ref[row:row + 1, :], cnt)
        c0_bits = pltpu.bitcast(jnp.exp(s0 - top0) / z, jnp.int32)
        cc_ref[h, 0] = pltpu.bitcast((c0_bits & ~COUNT_MASK) | cnt.astype(jnp.int32), F32)
        rank_ref[h, 0] = pltpu.bitcast(rank1.astype(BF16), jnp.uint32)
        e1_ref[h, 0] = pltpu.bitcast(jnp.exp(s1 - top1).astype(BF16), jnp.uint32)
        return carry

    lax.fori_loop(0, n_heads, head, 0)


def _route(scores, tb):
    nsk, bsz, nk, s = scores.shape
    n_heads = nsk // 2
    spec = lambda rows: pl.BlockSpec((n_heads, 1, rows, tb), lambda b, i: (0, b, 0, i))
    words = jax.ShapeDtypeStruct((n_heads, bsz, nk // 2, s), jnp.uint32)
    return pl.pallas_call(
        _route_kernel,
        grid=(bsz, s // tb),
        in_specs=[pl.BlockSpec((nsk, 1, nk, tb), lambda b, i: (0, b, 0, i))],
        out_specs=[spec(nk), spec(nk // 2), spec(nk // 2)],
        out_shape=[jax.ShapeDtypeStruct((n_heads, bsz, nk, s), F32), words, words],
        scratch_shapes=[pltpu.VMEM((2, PEER_TOPK, tb), F32),
                        pltpu.VMEM((ROUTE_ROWS, tb), F32),
                        pltpu.VMEM((ROUTE_ROWS, tb), F32),
                        pltpu.VMEM((2, nk, tb), F32)],
        compiler_params=_params("parallel", "parallel"),
    )(scores)


def _gelu(x):
    return 0.5 * x * (1.0 + lax.erf(x * math.sqrt(0.5)))


def _peer_kernel(pace_ref, h2_ref, u_ref, vt_ref, cc_ref, rank_ref, e1_ref, o_ref, act_ref, coef_ref,
                 *, groups, slice_lanes):
    j = pl.program_id(2)
    n_heads, _, nk, tb = cc_ref.shape
    te = groups * nk
    rows = BF16_SUBLANES
    words = rows // 2
    n_slices = tb // slice_lanes
    n_sub = 2 * n_slices
    part = lambda k: (k // n_slices, slice((k % n_slices) * slice_lanes, (k % n_slices + 1) * slice_lanes))

    def activations(k):
        slot, tok = part(k)
        act_ref[slot, :, tok] = _dot(u_ref[slot * te:(slot + 1) * te, :], h2_ref[0, tok, :], NT_DIMS)

    def apply(k):
        slot, tok = part(k)
        o_ref[0, :, tok] += _dot(vt_ref[slot], coef_ref[slot, :, tok])

    def coefficients(k):
        slot, tok = part(k)
        for gg in range(groups):
            g = (2 * j + slot) * groups + gg
            cc_rows = [cc_ref[h, 0, pl.ds(g, 1), :] for h in range(n_heads)]
            for l0 in range(tok.start, tok.stop, LANES):
                ln = slice(l0, l0 + LANES)
                cf = [None] * (nk // rows)
                for h, row in enumerate(cc_rows):
                    tile = jnp.broadcast_to(row[:, ln], (rows, LANES))
                    n0 = (pltpu.bitcast(tile, jnp.int32) & COUNT_MASK).astype(F32).astype(BF16)
                    c0 = tile.astype(BF16)
                    for rb in range(nk // rows):
                        wr = slice(rb * words, (rb + 1) * words)
                        rank = pltpu.bitcast(rank_ref[h, 0, wr, ln], BF16)
                        e1 = pltpu.bitcast(e1_ref[h, 0, wr, ln], BF16)
                        term = jnp.where(rank < n0, e1 * c0, 0.0)
                        cf[rb] = term if h == 0 else cf[rb] + term
                for rb in range(nk // rows):
                    er = slice(gg * nk + rb * rows, gg * nk + (rb + 1) * rows)
                    coef_ref[slot, er, ln] = cf[rb] * _gelu(act_ref[slot, er, ln]).astype(BF16)

    @pl.when(j == 0)
    def _():
        o_ref[...] = jnp.zeros_like(o_ref)

    paced = pace_ref[0] > 0
    pl.when(paced)(lambda: activations(0))
    for k in range(n_sub):
        @pl.when(paced)
        def _(k=k):
            if k + 1 < n_sub:
                activations(k + 1)
            if k > 0:
                apply(k - 1)
            coefficients(k)
    pl.when(paced)(lambda: apply(n_sub - 1))


def _peer(h2, u, vt3, cc, rank, e1, tb, groups):
    bsz, s, d = h2.shape
    n_heads, _, nk, _ = cc.shape
    te = groups * nk
    wspec = pl.BlockSpec((n_heads, 1, nk // 2, tb), lambda b, i, j: (0, b, 0, i))
    return pl.pallas_call(
        functools.partial(_peer_kernel, groups=groups, slice_lanes=min(2 * LANES, tb)),
        grid=(bsz, s // tb, u.shape[0] // (2 * te)),
        in_specs=[pl.BlockSpec(memory_space=pltpu.SMEM),
                  pl.BlockSpec((1, tb, d), lambda b, i, j: (b, i, 0)),
                  pl.BlockSpec((2 * te, d), lambda b, i, j: (j, 0)),
                  pl.BlockSpec((2, d, te), lambda b, i, j: (j, 0, 0)),
                  pl.BlockSpec((n_heads, 1, nk, tb), lambda b, i, j: (0, b, 0, i)),
                  wspec, wspec],
        out_specs=pl.BlockSpec((1, d, tb), lambda b, i, j: (b, 0, i)),
        out_shape=jax.ShapeDtypeStruct((bsz, d, s), F32),
        scratch_shapes=[pltpu.VMEM((2, te, tb), F32), pltpu.VMEM((2, te, tb), BF16)],
        compiler_params=_params("parallel", "parallel", "arbitrary"),
    )(jnp.ones((1,), jnp.int32), h2, u, vt3, cc, rank, e1)


def _residual_kernel(x1_ref, yt_ref, mod_ref, o_ref):
    o_ref[0] = x1_ref[0] + mod_ref[0, 5:6, :] * yt_ref[0].T


def _residual(x1, yt, mod, tm):
    bsz, s, d = x1.shape
    return pl.pallas_call(
        _residual_kernel,
        grid=(bsz, s // tm),
        in_specs=[pl.BlockSpec((1, tm, d), lambda b, i: (b, i, 0)),
                  pl.BlockSpec((1, d, tm), lambda b, i: (b, 0, i)),
                  pl.BlockSpec((1, N_MOD, d), lambda b, i: (b, 0, 0))],
        out_specs=pl.BlockSpec((1, tm, d), lambda b, i: (b, i, 0)),
        out_shape=jax.ShapeDtypeStruct((bsz, s, d), F32),
        compiler_params=_params("parallel", "parallel"),
    )(x1, yt, mod)


def _layer(x, c, w_ada, b_ada, g_norm1, w_in, b_igate, b_fgate, conv_w, g_q_sb, g_k_sb,
           g_out_sb, g_out_ml, w_out, g_norm2, w_q_peer, sub_keys, expert_u, expert_v):
    bsz, s, d = x.shape
    sb_heads, sb_hd = g_out_sb.shape
    ml_heads, ml_hd = g_out_ml.shape
    sb_width = sb_heads * sb_hd
    ml_width = ml_heads * ml_hd
    n_main = 3 * sb_width + 4 * ml_width
    peer_heads, _, n_keys, half = sub_keys.shape

    tm = min(512, s)
    tq = min(256, s)
    tb = min(1024, s)

    mod = _ada(c, w_ada, b_ada).reshape(bsz, N_MOD, d)
    p, gates = _inproj(x, mod, g_norm1.reshape(1, d), w_in[:, :n_main].astype(BF16),
                       w_in[:, n_main:].astype(BF16), tm)

    tile2 = lambda g: jnp.concatenate([g, g], axis=-1)
    ysb = _sb_attention(p, tile2(g_q_sb).reshape(1, 2 * sb_hd), tile2(g_k_sb).reshape(1, 2 * sb_hd),
                        g_out_sb.reshape(sb_heads // 2, 1, 2 * sb_hd), sb_heads, sb_hd, tq)
    yml = _mlstm(p, gates, conv_w, b_igate, b_fgate, g_out_ml, 3 * sb_width, ml_heads, ml_hd,
                 heads_per_step=2)

    w_keys = _fold_keys(sub_keys.reshape(2 * peer_heads, n_keys, half), w_q_peer)
    x1, h2, scores = _outproj(ysb, yml, x, mod, g_norm2.reshape(1, d), w_out.astype(BF16),
                              w_keys, n_keys, min(256, s))
    cc, rank, e1 = _route(scores, min(512, s))
    groups = 4
    te = groups * n_keys
    vt3 = expert_v.astype(BF16).reshape(-1, te, d).transpose(0, 2, 1)
    yt = _peer(h2, expert_u.astype(BF16), vt3, cc, rank, e1, tb, groups)
    return _residual(x1, yt, mod, tm)


def kernel(x, c, w_ada, b_ada, g_norm1, w_in, b_igate, b_fgate, conv_w, g_q_sb, g_k_sb, g_out_sb,
           g_out_ml, w_out, g_norm2, w_q_peer, sub_keys, expert_u, expert_v):
    params = (w_ada, b_ada, g_norm1, w_in, b_igate, b_fgate, conv_w, g_q_sb, g_k_sb, g_out_sb,
              g_out_ml, w_out, g_norm2, w_q_peer, sub_keys, expert_u, expert_v)
    for layer in range(w_ada.shape[0]):
        x = _layer(x, c, *(t[layer] for t in params))
    return x
```

```python
import functools
import math

import jax
import jax.numpy as jnp
from jax import lax
from jax.experimental import pallas as pl
from jax.experimental.pallas import tpu as pltpu

F32 = jnp.float32
BF16 = jnp.bfloat16

EPS = 1e-6
N_MOD = 6
MLSTM_CHUNK = 64
PEER_TOPK = 16
NEG_INF = float("-inf")
F32_EXP_UNDERFLOW = math.log(2.0 ** -126)

V7X_VMEM_BYTES = 64 * 1024 * 1024
VMEM_LIMIT = V7X_VMEM_BYTES - 12 * 1024 * 1024
LANES = 128
BF16_SUBLANES = 16

NT_DIMS = (((1,), (1,)), ((), ()))
NN_DIMS = (((1,), (0,)), ((), ()))
TN_DIMS = (((0,), (0,)), ((), ()))


def _params(*sem):
    return pltpu.CompilerParams(dimension_semantics=sem, vmem_limit_bytes=VMEM_LIMIT)


def _dot(a, b, dims=NN_DIMS):
    return lax.dot_general(a, b, dims, preferred_element_type=F32)


def _split_bf16(a):
    hi = a.astype(BF16)
    lo = (a - hi.astype(F32)).astype(BF16)
    return hi, lo


def _dot_f32(a, b, dims=NN_DIMS):
    ah, al = _split_bf16(a)
    bh, bl = _split_bf16(b)
    return _dot(ah, bh, dims) + (_dot(ah, bl, dims) + _dot(al, bh, dims))


def _dot_exact_rhs(a, b_bf16, dims=NN_DIMS):
    ah, al = _split_bf16(a)
    return _dot(ah, b_bf16, dims) + _dot(al, b_bf16, dims)


def _sigmoid(x):
    return 1.0 / (1.0 + jnp.exp(-x))


def _log_sigmoid_neg(z):
    return -(jnp.maximum(z, 0.0) + jnp.log(1.0 + jnp.exp(-jnp.abs(z))))


def _ada_kernel(c_ref, w_ref, b_ref, o_ref):
    c = c_ref[...]
    o_ref[...] = _dot_f32(c * _sigmoid(c), w_ref[...]) + b_ref[...]


def _ada(c, w, b):
    bsz, d = c.shape
    n = w.shape[1]
    tn = n // N_MOD
    return pl.pallas_call(
        _ada_kernel,
        grid=(n // tn,),
        in_specs=[pl.BlockSpec((bsz, d), lambda j: (0, 0)),
                  pl.BlockSpec((d, tn), lambda j: (0, j)),
                  pl.BlockSpec((1, tn), lambda j: (0, j))],
        out_specs=pl.BlockSpec((bsz, tn), lambda j: (0, j)),
        out_shape=jax.ShapeDtypeStruct((bsz, n), F32),
        compiler_params=_params("parallel"),
    )(c, w, b.reshape(1, n))


def _modulated_norm(x, g, scale, shift):
    y = x * lax.rsqrt(jnp.mean(x * x, axis=-1, keepdims=True) + EPS)
    return (y * g) * (1.0 + scale) + shift


def _inproj_kernel(x_ref, mod_ref, g_ref, w_ref, wg_ref, p_ref, gate_ref, *, n_chunk):
    h = _modulated_norm(x_ref[0], g_ref[...], mod_ref[0, 1:2, :], mod_ref[0, 0:1, :])
    hb = h.astype(BF16)
    for n0 in range(0, w_ref.shape[1], n_chunk):
        p_ref[0, :, n0:n0 + n_chunk] = _dot(hb, w_ref[:, n0:n0 + n_chunk]).astype(BF16)
    gate_ref[0] = _dot(hb, wg_ref[...])


def _inproj(x, mod, g, w_main, w_gate, tm):
    bsz, s, d = x.shape
    n = w_main.shape[1]
    ng = w_gate.shape[1]
    return pl.pallas_call(
        functools.partial(_inproj_kernel, n_chunk=512),
        grid=(bsz, s // tm),
        in_specs=[pl.BlockSpec((1, tm, d), lambda b, i: (b, i, 0)),
                  pl.BlockSpec((1, N_MOD, d), lambda b, i: (b, 0, 0)),
                  pl.BlockSpec((1, d), lambda b, i: (0, 0)),
                  pl.BlockSpec((d, n), lambda b, i: (0, 0)),
                  pl.BlockSpec((d, ng), lambda b, i: (0, 0))],
        out_specs=[pl.BlockSpec((1, tm, n), lambda b, i: (b, i, 0)),
                   pl.BlockSpec((1, tm, ng), lambda b, i: (b, i, 0))],
        out_shape=[jax.ShapeDtypeStruct((bsz, s, n), BF16),
                   jax.ShapeDtypeStruct((bsz, s, ng), F32)],
        compiler_params=_params("parallel", "parallel"),
    )(x, mod, g, w_main, w_gate)


def _pair_rms(x, g2, hd):
    lane = lax.broadcasted_iota(jnp.int32, x.shape, 1)
    sq = x * x
    s0 = jnp.sum(jnp.where(lane < hd, sq, 0.0), axis=-1, keepdims=True)
    s1 = jnp.sum(jnp.where(lane < hd, 0.0, sq), axis=-1, keepdims=True)
    inv = jnp.where(lane < hd, lax.rsqrt(s0 / hd + EPS), lax.rsqrt(s1 / hd + EPS))
    return x * inv * g2


def _sb_kernel(q_ref, k_ref, v_ref, gq_ref, gk_ref, go_ref, o_ref, kn_ref, acc_ref, run_ref,
               *, tq, hd):
    qi = pl.program_id(2)
    s = k_ref.shape[1]
    rows = 512 if s % 512 == 0 else tq

    @pl.when(qi == 0)
    def _():
        for r0 in range(0, s, rows):
            kk = k_ref[0, r0:r0 + rows, :].astype(F32)
            kn_ref[r0:r0 + rows, :] = _pair_rms(kk, gk_ref[...], hd).astype(BF16)

    qn = _pair_rms(q_ref[0].astype(F32), gq_ref[...], hd) * (hd ** -0.5)
    lane = lax.broadcasted_iota(jnp.int32, qn.shape, 1)
    q2 = jnp.concatenate([jnp.where(lane < hd, qn, 0.0), jnp.where(lane < hd, 0.0, qn)],
                         axis=0).astype(BF16)

    kw = min(2 * tq, s)
    r = lax.broadcasted_iota(jnp.int32, (kw, kw), 0)
    c = lax.broadcasted_iota(jnp.int32, (kw, kw), 1)
    suffix = jnp.where(r >= c, 1.0, 0.0).astype(BF16)
    first = jnp.maximum(qi - 1, 0) * tq if kw > tq else 0
    r2 = lax.broadcasted_iota(jnp.int32, (2 * tq, kw), 0)
    c2 = lax.broadcasted_iota(jnp.int32, (2 * tq, kw), 1)
    visible = first + c2 < qi * tq + jnp.where(r2 < tq, r2, r2 - tq)

    def sweep(k0, width, masked):
        if not isinstance(k0, int):
            k0 = pl.multiple_of(k0, tq)
        z = _dot(q2, kn_ref[pl.ds(k0, width), :], NT_DIMS)
        lk = _log_sigmoid_neg(z)
        if masked:
            lk = jnp.where(visible, lk, 0.0)
        cs = _dot_exact_rhs(lk, suffix[:width, :width])
        run = run_ref[...]
        w = jnp.exp(z + cs + run)
        if masked:
            w = jnp.where(visible, w, 0.0)
        acc_ref[...] += _dot(w.astype(BF16), v_ref[0, pl.ds(k0, width), :])
        run = run + cs[:, 0:1]
        run_ref[...] = run
        return jnp.max(run)

    acc_ref[...] = jnp.zeros_like(acc_ref)
    run_ref[...] = jnp.zeros_like(run_ref)
    z_bound = 1.02 * (hd ** 0.5) * jnp.max(jnp.abs(gq_ref[...])) * jnp.max(jnp.abs(gk_ref[...]))

    def more(state):
        j, top = state
        return jnp.logical_and(j >= 0, top + z_bound > F32_EXP_UNDERFLOW)

    lax.while_loop(more, lambda st: (st[0] - 1, sweep(st[0] * tq, tq, False)),
                   (qi - kw // tq, sweep(first, kw, True)))

    y = jnp.where(lane < hd, acc_ref[0:tq, :], acc_ref[tq:2 * tq, :])
    o_ref[0] = _pair_rms(y, go_ref[0], hd).astype(o_ref.dtype)


def _sb_attention(p, g_q2, g_k2, g_out2, n_heads, hd, tq):
    bsz, s, _ = p.shape
    w = 2 * hd
    npair = n_heads // 2
    return pl.pallas_call(
        functools.partial(_sb_kernel, tq=tq, hd=hd),
        grid=(bsz, npair, s // tq),
        in_specs=[pl.BlockSpec((1, tq, w), lambda b, h, i: (b, i, h)),
                  pl.BlockSpec((1, s, w), lambda b, h, i: (b, 0, npair + h)),
                  pl.BlockSpec((1, s, w), lambda b, h, i: (b, 0, 2 * npair + h)),
                  pl.BlockSpec((1, w), lambda b, h, i: (0, 0)),
                  pl.BlockSpec((1, w), lambda b, h, i: (0, 0)),
                  pl.BlockSpec((1, 1, w), lambda b, h, i: (h, 0, 0))],
        out_specs=pl.BlockSpec((1, tq, w), lambda b, h, i: (b, i, h)),
        out_shape=jax.ShapeDtypeStruct((bsz, s, n_heads * hd), BF16),
        scratch_shapes=[pltpu.VMEM((s, w), BF16), pltpu.VMEM((2 * tq, w), F32),
                        pltpu.VMEM((2 * tq, 1), F32)],
        compiler_params=_params("parallel", "parallel", "arbitrary"),
    )(p, p, p, g_q2, g_k2, g_out2)


def _ml_kernel(q_ref, k_ref, v_ref, o_ref, gate_ref, cwq_ref, cwk_ref, big_ref, bfg_ref,
               gout_ref, y_ref, *, heads_per_step, dh, conv_width):
    L = MLSTM_CHUNK
    s = q_ref.shape[1]
    n_heads = gate_ref.shape[2] // 2
    hp = pl.program_id(1)
    halo = BF16_SUBLANES

    r = lax.broadcasted_iota(jnp.int32, (L, L), 0)
    c = lax.broadcasted_iota(jnp.int32, (L, L), 1)
    eye = r == c
    causal = c <= r
    prefix = jnp.where(causal, 1.0, 0.0).astype(BF16)
    glane = lax.broadcasted_iota(jnp.int32, (1, 2 * n_heads), 1)

    def conv_silu(src_ref, w_ref, ci):
        if isinstance(ci, int):
            assert ci == 0
            ext = jnp.concatenate([jnp.zeros((halo, src_ref.shape[2]), F32),
                                   src_ref[0, 0:L, :].astype(F32)], axis=0)
        else:
            start = pl.multiple_of(ci * L - halo, halo)
            ext = src_ref[0, pl.ds(start, L + halo), :].astype(F32)
        y = jnp.zeros((L, ext.shape[1]), F32)
        for j in range(conv_width):
            off = halo - (conv_width - 1) + j
            y = y + ext[off:off + L, :] * w_ref[j:j + 1, :]
        return y * _sigmoid(y)

    def to_row(col):
        return jnp.sum(jnp.where(eye, col, 0.0), axis=0, keepdims=True)

    def chunk(ci, carry):
        t0 = 0 if isinstance(ci, int) else pl.multiple_of(ci * L, L)
        g = gate_ref[0, pl.ds(t0, L), :]
        i_all = g + big_ref[...]
        lf_hi, lf_lo = _split_bf16(_log_sigmoid_neg(-(g + bfg_ref[...])))
        a_all = _dot(prefix, lf_hi) + _dot(prefix, lf_lo)
        qc = conv_silu(q_ref, cwq_ref, ci)
        kc = conv_silu(k_ref, cwk_ref, ci) * (dh ** -0.5)
        new_carry = []
        for hh in range(heads_per_step):
            ct, n_row, m_st = carry[hh]
            col = hp * heads_per_step + hh
            pick_i = glane == col
            pick_f = glane == (n_heads + col)
            i_col = jnp.sum(jnp.where(pick_i, i_all, 0.0), axis=1, keepdims=True)
            a_col = jnp.sum(jnp.where(pick_f, a_all, 0.0), axis=1, keepdims=True)
            a_row = to_row(a_col)
            i_row = to_row(i_col)
            sl = slice(hh * dh, (hh + 1) * dh)
            q_h = qc[:, sl].astype(BF16)
            k_h = kc[:, sl]
            k_hb = k_h.astype(BF16)
            v_h = v_ref[0, pl.ds(t0, L), sl]

            log_d = jnp.where(causal, a_col - a_row + i_row, NEG_INF)
            log_inter = a_col + m_st
            m_row = jnp.maximum(jnp.max(log_d, axis=-1, keepdims=True), log_inter)
            w_intra = jnp.exp(log_d - m_row)
            w_inter = jnp.exp(log_inter - m_row)
            s_qk = _dot(q_h, k_hb, NT_DIMS) * w_intra
            num = _dot(s_qk.astype(BF16), v_h) + w_inter * _dot(q_h, ct.astype(BF16))
            qn = jnp.sum(q_h.astype(F32) * n_row, axis=-1, keepdims=True)
            den = jnp.sum(s_qk, axis=-1, keepdims=True) + w_inter * qn
            h = num / jnp.maximum(jnp.abs(den), jnp.exp(-m_row))

            a_end = a_col[L - 1:L, :]
            log_w = a_end - a_col + i_col
            m_new = jnp.maximum(a_end + m_st, jnp.max(log_w, axis=0, keepdims=True))
            w_s = jnp.exp(log_w - m_new)
            decay = jnp.exp(a_end + m_st - m_new)
            kw = k_h * w_s
            ct_new = decay * ct + _dot(kw.astype(BF16), v_h, TN_DIMS)
            n_new = decay * n_row + jnp.sum(kw, axis=0, keepdims=True)
            new_carry.append((ct_new, n_new, m_new))

            ms = jnp.mean(h * h, axis=-1, keepdims=True)
            og = _sigmoid(o_ref[0, pl.ds(t0, L), sl].astype(F32))
            y = h * lax.rsqrt(ms + EPS) * gout_ref[0, hh:hh + 1, :] * og
            y_ref[0, pl.ds(t0, L), sl] = y.astype(y_ref.dtype)
        return tuple(new_carry)

    init = tuple((jnp.zeros((dh, dh), F32), jnp.zeros((1, dh), F32), jnp.zeros((1, 1), F32))
                 for _ in range(heads_per_step))
    lax.fori_loop(1, s // L, chunk, chunk(0, init))


def _mlstm(p, gates, conv_w, b_i, b_f, g_out, col0, n_heads, dh, heads_per_step):
    bsz, s, _ = p.shape
    w = heads_per_step * dh
    nstep = n_heads // heads_per_step
    width = n_heads * dh
    cb = col0 // w
    wb = width // w
    conv_width = conv_w.shape[0]
    bias = jnp.concatenate([b_i, b_f]).reshape(1, 2 * n_heads)
    lane = jnp.arange(2 * n_heads) < n_heads
    big = jnp.where(lane, bias, 0.0)
    bfg = jnp.where(lane, 0.0, bias)
    kern = functools.partial(_ml_kernel, heads_per_step=heads_per_step, dh=dh, conv_width=conv_width)
    seq = lambda off: pl.BlockSpec((1, s, w), lambda b, h: (b, 0, cb + off + h))
    return pl.pallas_call(
        kern,
        grid=(bsz, nstep),
        in_specs=[seq(0), seq(wb), seq(2 * wb), seq(3 * wb),
                  pl.BlockSpec((1, s, 2 * n_heads), lambda b, h: (b, 0, 0)),
                  pl.BlockSpec((conv_width, w), lambda b, h: (0, h)),
                  pl.BlockSpec((conv_width, w), lambda b, h: (0, wb + h)),
                  pl.BlockSpec((1, 2 * n_heads), lambda b, h: (0, 0)),
                  pl.BlockSpec((1, 2 * n_heads), lambda b, h: (0, 0)),
                  pl.BlockSpec((1, heads_per_step, dh), lambda b, h: (h, 0, 0))],
        out_specs=pl.BlockSpec((1, s, w), lambda b, h: (b, 0, h)),
        out_shape=jax.ShapeDtypeStruct((bsz, s, width), BF16),
        compiler_params=_params("parallel", "parallel"),
    )(p, p, p, p, gates, conv_w, conv_w, big, bfg,
      g_out.reshape(nstep, heads_per_step, dh))


def _fold_kernel(keys_ref, wq_ref, o_ref):
    o_ref[...] = _dot_f32(keys_ref[0], wq_ref[...], NT_DIMS).astype(BF16)


def _fold_keys(keys, w_q):
    nsk, nk, half = keys.shape
    d = w_q.shape[0]
    return pl.pallas_call(
        _fold_kernel,
        grid=(nsk,),
        in_specs=[pl.BlockSpec((1, nk, half), lambda i: (i, 0, 0)),
                  pl.BlockSpec((d, half), lambda i: (0, i))],
        out_specs=pl.BlockSpec((nk, d), lambda i: (i, 0)),
        out_shape=jax.ShapeDtypeStruct((nsk * nk, d), BF16),
        compiler_params=_params("parallel"),
    )(keys, w_q)


def _outproj_kernel(ysb_ref, yml_ref, x_ref, mod_ref, g_ref, wo_ref, wk_ref, x1_ref, h2_ref, sc_ref):
    wsb = ysb_ref.shape[2]
    mix = _dot(ysb_ref[0], wo_ref[0:wsb, :]) + _dot(yml_ref[0], wo_ref[wsb:, :])
    x1 = x_ref[0] + mod_ref[0, 2:3, :] * mix
    x1_ref[0] = x1
    h2 = _modulated_norm(x1, g_ref[...], mod_ref[0, 4:5, :], mod_ref[0, 3:4, :]).astype(BF16)
    h2_ref[0] = h2
    nk = sc_ref.shape[2]
    scores = _dot(wk_ref[...], h2, NT_DIMS)
    for i in range(sc_ref.shape[0]):
        sc_ref[i, 0] = scores[i * nk:(i + 1) * nk, :]


def _outproj(ysb, yml, x, mod, g, w_out, w_keys, nk, tm):
    bsz, s, d = x.shape
    nsk = w_keys.shape[0] // nk
    return pl.pallas_call(
        _outproj_kernel,
        grid=(bsz, s // tm),
        in_specs=[pl.BlockSpec((1, tm, ysb.shape[2]), lambda b, i: (b, i, 0)),
                  pl.BlockSpec((1, tm, yml.shape[2]), lambda b, i: (b, i, 0)),
                  pl.BlockSpec((1, tm, d), lambda b, i: (b, i, 0)),
                  pl.BlockSpec((1, N_MOD, d), lambda b, i: (b, 0, 0)),
                  pl.BlockSpec((1, d), lambda b, i: (0, 0)),
                  pl.BlockSpec(w_out.shape, lambda b, i: (0, 0)),
                  pl.BlockSpec(w_keys.shape, lambda b, i: (0, 0))],
        out_specs=[pl.BlockSpec((1, tm, d), lambda b, i: (b, i, 0)),
                   pl.BlockSpec((1, tm, d), lambda b, i: (b, i, 0)),
                   pl.BlockSpec((nsk, 1, nk, tm), lambda b, i: (0, b, 0, i))],
        out_shape=[jax.ShapeDtypeStruct((bsz, s, d), F32),
                   jax.ShapeDtypeStruct((bsz, s, d), BF16),
                   jax.ShapeDtypeStruct((nsk, bsz, nk, s), F32)],
        compiler_params=_params("parallel", "parallel"),
    )(ysb, yml, x, mod, g, w_out, w_keys)


def _extract_top(x, k, pos):
    n = x.shape[0]
    rank = jnp.full(x.shape, float(k), F32)
    tops = []
    for a in range(k):
        m = jnp.max(x, axis=0, keepdims=True)
        first = jnp.min(jnp.where(x == m, pos, float(n)), axis=0, keepdims=True)
        pick = pos == first
        rank = jnp.where(pick, float(a), rank)
        x = jnp.where(pick, NEG_INF, x)
        tops.append(m)
    return tops, rank


def _extract_top_fast(x, k):
    rank = jnp.full(x.shape, float(k), F32)
    tops = []
    for a in range(k):
        m = jnp.max(x, axis=0, keepdims=True)
        pick = x == m
        rank = jnp.where(pick, float(a), rank)
        x = jnp.where(pick, NEG_INF, x)
        tops.append(m)
    return tops, rank


SUBLANES = 8
COUNT_BITS = 5
COUNT_MASK = (1 << COUNT_BITS) - 1
ROUTE_ROWS = PEER_TOPK + SUBLANES * (PEER_TOPK // 2 - 1) + PEER_TOPK // 2


def _route_kernel(sc_ref, cc_ref, rank_ref, e1_ref, top_ref, cand_ref, sel_ref, rk_ref):
    n_heads = cc_ref.shape[0]
    nk, tb = sc_ref.shape[2], sc_ref.shape[3]
    K = PEER_TOPK
    assert K == 2 * SUBLANES
    pos = lax.broadcasted_iota(jnp.int32, (nk, tb), 0).astype(F32)
    cpos = lax.broadcasted_iota(jnp.int32, (ROUTE_ROWS, tb), 0).astype(F32)
    sub = lax.broadcasted_iota(jnp.int32, (SUBLANES, tb), 0)
    group = lambda a: slice(K + SUBLANES * (a - 1), K + SUBLANES * a)
    tail = slice(K + SUBLANES * (K // 2 - 1), ROUTE_ROWS)

    def tied(rank):
        n_ranked = jnp.sum(jnp.where(rank < float(K), 1.0, 0.0), axis=0, keepdims=True)
        return jnp.max(n_ranked) > float(K)

    def head(h, carry):
        s0 = sc_ref[2 * h, 0]
        s1 = sc_ref[2 * h + 1, 0]
        def keep(p, tops, rank):
            for a in range(K):
                top_ref[p, a:a + 1, :] = tops[a]
            rk_ref[p] = rank

        fast = [_extract_top_fast(x, K) for x in (s0, s1)]
        for p in range(2):
            keep(p, *fast[p])
        retry = [tied(fast[p][1]) for p in range(2)]
        for p, x in ((0, s0), (1, s1)):
            pl.when(retry[p])(lambda p=p, x=x: keep(p, *_extract_top(x, K, pos)))
        rank0, rank1 = rk_ref[0], rk_ref[1]
        top0, top1 = top_ref[0, 0:1, :], top_ref[1, 0:1, :]

        cand_ref[0:K, :] = top0 + top_ref[1]
        head1 = top_ref[1, 0:SUBLANES, :]
        for a in range(1, K // 2):
            cand_ref[group(a), :] = jnp.where(sub < K // (a + 1), top_ref[0, a:a + 1, :] + head1, NEG_INF)
        cand_ref[tail, :] = top_ref[0, K // 2:K, :] + top1
        cand = cand_ref[...]

        def keep_sel(pick_round):
            sel_ref[...] = jnp.where(pick_round < float(K), 1.0, 0.0)

        _, pick_round = _extract_top_fast(cand, K)
        keep_sel(pick_round)
        pl.when(tied(pick_round))(lambda: keep_sel(_extract_top(cand_ref[...], K, cpos)[1]))
        z = jnp.sum(sel_ref[...] * jnp.exp(cand - (top0 + top1)), axis=0, keepdims=True)

        cnt = jnp.where(rank0 == 0.0, jnp.sum(sel_ref[0:K, :], axis=0, keepdims=True), 0.0)
        for a in range(1, K // 2):
            cnt = jnp.where(rank0 == float(a), jnp.sum(sel_ref[group(a), :], axis=0, keepdims=True), cnt)
        for a in range(K // 2, K):
            row = tail.start + a - K // 2
            cnt = jnp.where(rank0 == float(a), sel_ref[row:row + 1, :], cnt)
        c0_bits = pltpu.bitcast(jnp.exp(s0 - top0) / z, jnp.int32)
        cc_ref[h, 0] = pltpu.bitcast((c0_bits & ~COUNT_MASK) | cnt.astype(jnp.int32), F32)
        rank_ref[h, 0] = pltpu.bitcast(rank1.astype(BF16), jnp.uint32)
        e1_ref[h, 0] = pltpu.bitcast(jnp.exp(s1 - top1).astype(BF16), jnp.uint32)
        return carry

    lax.fori_loop(0, n_heads, head, 0)


def _route(scores, tb):
    nsk, bsz, nk, s = scores.shape
    n_heads = nsk // 2
    spec = lambda rows: pl.BlockSpec((n_heads, 1, rows, tb), lambda b, i: (0, b, 0, i))
    words = jax.ShapeDtypeStruct((n_heads, bsz, nk // 2, s), jnp.uint32)
    return pl.pallas_call(
        _route_kernel,
        grid=(bsz, s // tb),
        in_specs=[pl.BlockSpec((nsk, 1, nk, tb), lambda b, i: (0, b, 0, i))],
        out_specs=[spec(nk), spec(nk // 2), spec(nk // 2)],
        out_shape=[jax.ShapeDtypeStruct((n_heads, bsz, nk, s), F32), words, words],
        scratch_shapes=[pltpu.VMEM((2, PEER_TOPK, tb), F32),
                        pltpu.VMEM((ROUTE_ROWS, tb), F32),
                        pltpu.VMEM((ROUTE_ROWS, tb), F32),
                        pltpu.VMEM((2, nk, tb), F32)],
        compiler_params=_params("parallel", "parallel"),
    )(scores)


def _gelu(x):
    return 0.5 * x * (1.0 + lax.erf(x * math.sqrt(0.5)))


def _peer_kernel(pace_ref, h2_ref, u_ref, vt_ref, cc_ref, rank_ref, e1_ref, o_ref, act_ref, coef_ref,
                 *, groups, slice_lanes):
    j = pl.program_id(2)
    n_heads, _, nk, tb = cc_ref.shape
    te = groups * nk
    rows = BF16_SUBLANES
    words = rows // 2
    n_slices = tb // slice_lanes
    n_sub = 2 * n_slices
    part = lambda k: (k // n_slices, slice((k % n_slices) * slice_lanes, (k % n_slices + 1) * slice_lanes))

    def activations(k):
        slot, tok = part(k)
        act_ref[slot, :, tok] = _dot(u_ref[slot * te:(slot + 1) * te, :], h2_ref[0, tok, :], NT_DIMS)

    def apply(k):
        slot, tok = part(k)
        o_ref[0, :, tok] += _dot(vt_ref[slot], coef_ref[slot, :, tok])

    def coefficients(k):
        slot, tok = part(k)
        assert 2 * groups == SUBLANES
        g0 = pl.multiple_of(j * (2 * groups), 2 * groups)
        scale = [cc_ref[h, 0, pl.ds(g0, 2 * groups), tok] for h in range(n_heads)]
        count = [(pltpu.bitcast(t, jnp.int32) & COUNT_MASK).astype(F32) for t in scale]
        for gg in range(groups):
            row = slot * groups + gg
            for l0 in range(tok.start, tok.stop, LANES):
                ln = slice(l0, l0 + LANES)
                lt = slice(l0 - tok.start, l0 - tok.start + LANES)
                cf = [None] * (nk // rows)
                for h in range(n_heads):
                    spread = lambda t: jnp.broadcast_to(t[row:row + 1, lt], (rows, LANES)).astype(BF16)
                    n0, c0 = spread(count[h]), spread(scale[h])
                    for rb in range(nk // rows):
                        wr = slice(rb * words, (rb + 1) * words)
                        rank = pltpu.bitcast(rank_ref[h, 0, wr, ln], BF16)
                        e1 = pltpu.bitcast(e1_ref[h, 0, wr, ln], BF16)
                        term = jnp.where(rank < n0, e1 * c0, 0.0)
                        cf[rb] = term if h == 0 else cf[rb] + term
                for rb in range(nk // rows):
                    er = slice(gg * nk + rb * rows, gg * nk + (rb + 1) * rows)
                    coef_ref[slot, er, ln] = cf[rb] * _gelu(act_ref[slot, er, ln]).astype(BF16)

    @pl.when(j == 0)
    def _():
        o_ref[...] = jnp.zeros_like(o_ref)

    paced = pace_ref[0] > 0
    pl.when(paced)(lambda: activations(0))
    for k in range(n_sub):
        @pl.when(paced)
        def _(k=k):
            if k + 1 < n_sub:
                activations(k + 1)
            if k > 0:
                apply(k - 1)
            coefficients(k)
    pl.when(paced)(lambda: apply(n_sub - 1))


def _peer(h2, u, vt3, cc, rank, e1, tb, groups):
    bsz, s, d = h2.shape
    n_heads, _, nk, _ = cc.shape
    te = groups * nk
    wspec = pl.BlockSpec((n_heads, 1, nk // 2, tb), lambda b, i, j: (0, b, 0, i))
    return pl.pallas_call(
        functools.partial(_peer_kernel, groups=groups, slice_lanes=min(2 * LANES, tb)),
        grid=(bsz, s // tb, u.shape[0] // (2 * te)),
        in_specs=[pl.BlockSpec(memory_space=pltpu.SMEM),
                  pl.BlockSpec((1, tb, d), lambda b, i, j: (b, i, 0)),
                  pl.BlockSpec((2 * te, d), lambda b, i, j: (j, 0)),
                  pl.BlockSpec((2, d, te), lambda b, i, j: (j, 0, 0)),
                  pl.BlockSpec((n_heads, 1, nk, tb), lambda b, i, j: (0, b, 0, i)),
                  wspec, wspec],
        out_specs=pl.BlockSpec((1, d, tb), lambda b, i, j: (b, 0, i)),
        out_shape=jax.ShapeDtypeStruct((bsz, d, s), F32),
        scratch_shapes=[pltpu.VMEM((2, te, tb), F32), pltpu.VMEM((2, te, tb), BF16)],
        compiler_params=_params("parallel", "parallel", "arbitrary"),
    )(jnp.ones((1,), jnp.int32), h2, u, vt3, cc, rank, e1)


def _residual_kernel(x1_ref, yt_ref, mod_ref, o_ref):
    o_ref[0] = x1_ref[0] + mod_ref[0, 5:6, :] * yt_ref[0].T


def _residual(x1, yt, mod, tm):
    bsz, s, d = x1.shape
    return pl.pallas_call(
        _residual_kernel,
        grid=(bsz, s // tm),
        in_specs=[pl.BlockSpec((1, tm, d), lambda b, i: (b, i, 0)),
                  pl.BlockSpec((1, d, tm), lambda b, i: (b, 0, i)),
                  pl.BlockSpec((1, N_MOD, d), lambda b, i: (b, 0, 0))],
        out_specs=pl.BlockSpec((1, tm, d), lambda b, i: (b, i, 0)),
        out_shape=jax.ShapeDtypeStruct((bsz, s, d), F32),
        compiler_params=_params("parallel", "parallel"),
    )(x1, yt, mod)


def _layer(x, c, w_ada, b_ada, g_norm1, w_in, b_igate, b_fgate, conv_w, g_q_sb, g_k_sb,
           g_out_sb, g_out_ml, w_out, g_norm2, w_q_peer, sub_keys, expert_u, expert_v):
    bsz, s, d = x.shape
    sb_heads, sb_hd = g_out_sb.shape
    ml_heads, ml_hd = g_out_ml.shape
    sb_width = sb_heads * sb_hd
    ml_width = ml_heads * ml_hd
    n_main = 3 * sb_width + 4 * ml_width
    peer_heads, _, n_keys, half = sub_keys.shape

    tm = min(512, s)
    tq = min(256, s)
    tb = min(1024, s)

    mod = _ada(c, w_ada, b_ada).reshape(bsz, N_MOD, d)
    p, gates = _inproj(x, mod, g_norm1.reshape(1, d), w_in[:, :n_main].astype(BF16),
                       w_in[:, n_main:].astype(BF16), tm)

    tile2 = lambda g: jnp.concatenate([g, g], axis=-1)
    ysb = _sb_attention(p, tile2(g_q_sb).reshape(1, 2 * sb_hd), tile2(g_k_sb).reshape(1, 2 * sb_hd),
                        g_out_sb.reshape(sb_heads // 2, 1, 2 * sb_hd), sb_heads, sb_hd, tq)
    yml = _mlstm(p, gates, conv_w, b_igate, b_fgate, g_out_ml, 3 * sb_width, ml_heads, ml_hd,
                 heads_per_step=ml_heads)

    w_keys = _fold_keys(sub_keys.reshape(2 * peer_heads, n_keys, half), w_q_peer)
    x1, h2, scores = _outproj(ysb, yml, x, mod, g_norm2.reshape(1, d), w_out.astype(BF16),
                              w_keys, n_keys, min(256, s))
    cc, rank, e1 = _route(scores, min(512, s))
    groups = 4
    te = groups * n_keys
    vt3 = expert_v.astype(BF16).reshape(-1, te, d).transpose(0, 2, 1)
    yt = _peer(h2, expert_u.astype(BF16), vt3, cc, rank, e1, tb, groups)
    return _residual(x1, yt, mod, tm)


def kernel(x, c, w_ada, b_ada, g_norm1, w_in, b_igate, b_fgate, conv_w, g_q_sb, g_k_sb, g_out_sb,
           g_out_ml, w_out, g_norm2, w_q_peer, sub_keys, expert_u, expert_v):
    params = (w_ada, b_ada, g_norm1, w_in, b_igate, b_fgate, conv_w, g_q_sb, g_k_sb, g_out_sb,
              g_out_ml, w_out, g_norm2, w_q_peer, sub_keys, expert_u, expert_v)
    for layer in range(w_ada.shape[0]):
        x = _layer(x, c, *(t[layer] for t in params))
    return x
```

```python
import functools
import math

import jax
import jax.numpy as jnp
from jax import lax
from jax.experimental import pallas as pl
from jax.experimental.pallas import tpu as pltpu

F32 = jnp.float32
BF16 = jnp.bfloat16

EPS = 1e-6
N_MOD = 6
MLSTM_CHUNK = 64
PEER_TOPK = 16
NEG_INF = float("-inf")
F32_EXP_UNDERFLOW = math.log(2.0 ** -126)

V7X_VMEM_BYTES = 64 * 1024 * 1024
VMEM_LIMIT = V7X_VMEM_BYTES - 12 * 1024 * 1024
LANES = 128
BF16_SUBLANES = 16

NT_DIMS = (((1,), (1,)), ((), ()))
NN_DIMS = (((1,), (0,)), ((), ()))
TN_DIMS = (((0,), (0,)), ((), ()))


def _params(*sem):
    return pltpu.CompilerParams(dimension_semantics=sem, vmem_limit_bytes=VMEM_LIMIT)


def _dot(a, b, dims=NN_DIMS):
    return lax.dot_general(a, b, dims, preferred_element_type=F32)


def _split_bf16(a):
    hi = a.astype(BF16)
    lo = (a - hi.astype(F32)).astype(BF16)
    return hi, lo


def _dot_f32(a, b, dims=NN_DIMS):
    ah, al = _split_bf16(a)
    bh, bl = _split_bf16(b)
    return _dot(ah, bh, dims) + (_dot(ah, bl, dims) + _dot(al, bh, dims))


def _dot_exact_rhs(a, b_bf16, dims=NN_DIMS):
    ah, al = _split_bf16(a)
    return _dot(ah, b_bf16, dims) + _dot(al, b_bf16, dims)


def _sigmoid(x):
    return 1.0 / (1.0 + jnp.exp(-x))


def _log_sigmoid_neg(z):
    return -(jnp.maximum(z, 0.0) + jnp.log(1.0 + jnp.exp(-jnp.abs(z))))


def _ada_kernel(c_ref, w_ref, b_ref, o_ref):
    c = c_ref[...]
    o_ref[...] = _dot_f32(c * _sigmoid(c), w_ref[...]) + b_ref[...]


def _ada(c, w, b):
    bsz, d = c.shape
    n = w.shape[1]
    tn = n // N_MOD
    return pl.pallas_call(
        _ada_kernel,
        grid=(n // tn,),
        in_specs=[pl.BlockSpec((bsz, d), lambda j: (0, 0)),
                  pl.BlockSpec((d, tn), lambda j: (0, j)),
                  pl.BlockSpec((1, tn), lambda j: (0, j))],
        out_specs=pl.BlockSpec((bsz, tn), lambda j: (0, j)),
        out_shape=jax.ShapeDtypeStruct((bsz, n), F32),
        compiler_params=_params("parallel"),
    )(c, w, b.reshape(1, n))


def _modulated_norm(x, g, scale, shift):
    y = x * lax.rsqrt(jnp.mean(x * x, axis=-1, keepdims=True) + EPS)
    return (y * g) * (1.0 + scale) + shift


def _inproj_kernel(x_ref, mod_ref, g_ref, w_ref, wg_ref, p_ref, gate_ref, *, n_chunk):
    h = _modulated_norm(x_ref[0], g_ref[...], mod_ref[0, 1:2, :], mod_ref[0, 0:1, :])
    hb = h.astype(BF16)
    for n0 in range(0, w_ref.shape[1], n_chunk):
        p_ref[0, :, n0:n0 + n_chunk] = _dot(hb, w_ref[:, n0:n0 + n_chunk]).astype(BF16)
    gate_ref[0] = _dot(hb, wg_ref[...])


def _inproj(x, mod, g, w_main, w_gate, tm):
    bsz, s, d = x.shape
    n = w_main.shape[1]
    ng = w_gate.shape[1]
    return pl.pallas_call(
        functools.partial(_inproj_kernel, n_chunk=512),
        grid=(bsz, s // tm),
        in_specs=[pl.BlockSpec((1, tm, d), lambda b, i: (b, i, 0)),
                  pl.BlockSpec((1, N_MOD, d), lambda b, i: (b, 0, 0)),
                  pl.BlockSpec((1, d), lambda b, i: (0, 0)),
                  pl.BlockSpec((d, n), lambda b, i: (0, 0)),
                  pl.BlockSpec((d, ng), lambda b, i: (0, 0))],
        out_specs=[pl.BlockSpec((1, tm, n), lambda b, i: (b, i, 0)),
                   pl.BlockSpec((1, tm, ng), lambda b, i: (b, i, 0))],
        out_shape=[jax.ShapeDtypeStruct((bsz, s, n), BF16),
                   jax.ShapeDtypeStruct((bsz, s, ng), F32)],
        compiler_params=_params("parallel", "parallel"),
    )(x, mod, g, w_main, w_gate)


def _pair_rms(x, g2, hd):
    lane = lax.broadcasted_iota(jnp.int32, x.shape, 1)
    sq = x * x
    s0 = jnp.sum(jnp.where(lane < hd, sq, 0.0), axis=-1, keepdims=True)
    s1 = jnp.sum(jnp.where(lane < hd, 0.0, sq), axis=-1, keepdims=True)
    inv = jnp.where(lane < hd, lax.rsqrt(s0 / hd + EPS), lax.rsqrt(s1 / hd + EPS))
    return x * inv * g2


def _sb_kernel(q_ref, k_ref, v_ref, gq_ref, gk_ref, go_ref, o_ref, kn_ref, acc_ref, run_ref,
               *, tq, hd):
    qi = pl.program_id(2)
    s = k_ref.shape[1]
    rows = 512 if s % 512 == 0 else tq

    @pl.when(qi == 0)
    def _():
        for r0 in range(0, s, rows):
            kk = k_ref[0, r0:r0 + rows, :].astype(F32)
            kn_ref[r0:r0 + rows, :] = _pair_rms(kk, gk_ref[...], hd).astype(BF16)

    qn = _pair_rms(q_ref[0].astype(F32), gq_ref[...], hd) * (hd ** -0.5)
    lane = lax.broadcasted_iota(jnp.int32, qn.shape, 1)
    q2 = jnp.concatenate([jnp.where(lane < hd, qn, 0.0), jnp.where(lane < hd, 0.0, qn)],
                         axis=0).astype(BF16)

    kw = min(2 * tq, s)
    r = lax.broadcasted_iota(jnp.int32, (kw, kw), 0)
    c = lax.broadcasted_iota(jnp.int32, (kw, kw), 1)
    suffix = jnp.where(r >= c, 1.0, 0.0).astype(BF16)
    first = jnp.maximum(qi - 1, 0) * tq if kw > tq else 0
    r2 = lax.broadcasted_iota(jnp.int32, (2 * tq, kw), 0)
    c2 = lax.broadcasted_iota(jnp.int32, (2 * tq, kw), 1)
    visible = first + c2 < qi * tq + jnp.where(r2 < tq, r2, r2 - tq)

    def sweep(k0, width, masked):
        if not isinstance(k0, int):
            k0 = pl.multiple_of(k0, tq)
        z = _dot(q2, kn_ref[pl.ds(k0, width), :], NT_DIMS)
        lk = _log_sigmoid_neg(z)
        if masked:
            lk = jnp.where(visible, lk, 0.0)
        cs = _dot_exact_rhs(lk, suffix[:width, :width])
        run = run_ref[...]
        w = jnp.exp(z + cs + run)
        if masked:
            w = jnp.where(visible, w, 0.0)
        acc_ref[...] += _dot(w.astype(BF16), v_ref[0, pl.ds(k0, width), :])
        run = run + cs[:, 0:1]
        run_ref[...] = run
        return jnp.max(run)

    acc_ref[...] = jnp.zeros_like(acc_ref)
    run_ref[...] = jnp.zeros_like(run_ref)
    z_bound = 1.02 * (hd ** 0.5) * jnp.max(jnp.abs(gq_ref[...])) * jnp.max(jnp.abs(gk_ref[...]))

    def more(state):
        j, top = state
        return jnp.logical_and(j >= 0, top + z_bound > F32_EXP_UNDERFLOW)

    lax.while_loop(more, lambda st: (st[0] - 1, sweep(st[0] * tq, tq, False)),
                   (qi - kw // tq, sweep(first, kw, True)))

    y = jnp.where(lane < hd, acc_ref[0:tq, :], acc_ref[tq:2 * tq, :])
    o_ref[0] = _pair_rms(y, go_ref[0], hd).astype(o_ref.dtype)


def _sb_attention(p, g_q2, g_k2, g_out2, n_heads, hd, tq):
    bsz, s, _ = p.shape
    w = 2 * hd
    npair = n_heads // 2
    return pl.pallas_call(
        functools.partial(_sb_kernel, tq=tq, hd=hd),
        grid=(bsz, npair, s // tq),
        in_specs=[pl.BlockSpec((1, tq, w), lambda b, h, i: (b, i, h)),
                  pl.BlockSpec((1, s, w), lambda b, h, i: (b, 0, npair + h)),
                  pl.BlockSpec((1, s, w), lambda b, h, i: (b, 0, 2 * npair + h)),
                  pl.BlockSpec((1, w), lambda b, h, i: (0, 0)),
                  pl.BlockSpec((1, w), lambda b, h, i: (0, 0)),
                  pl.BlockSpec((1, 1, w), lambda b, h, i: (h, 0, 0))],
        out_specs=pl.BlockSpec((1, tq, w), lambda b, h, i: (b, i, h)),
        out_shape=jax.ShapeDtypeStruct((bsz, s, n_heads * hd), BF16),
        scratch_shapes=[pltpu.VMEM((s, w), BF16), pltpu.VMEM((2 * tq, w), F32),
                        pltpu.VMEM((2 * tq, 1), F32)],
        compiler_params=_params("parallel", "parallel", "arbitrary"),
    )(p, p, p, g_q2, g_k2, g_out2)


def _ml_kernel(q_ref, k_ref, v_ref, o_ref, gate_ref, cwq_ref, cwk_ref, big_ref, bfg_ref,
               gout_ref, y_ref, *, heads_per_step, dh, conv_width):
    L = MLSTM_CHUNK
    s = q_ref.shape[1]
    n_heads = gate_ref.shape[2] // 2
    hp = pl.program_id(1)
    halo = BF16_SUBLANES

    r = lax.broadcasted_iota(jnp.int32, (L, L), 0)
    c = lax.broadcasted_iota(jnp.int32, (L, L), 1)
    eye = r == c
    causal = c <= r
    prefix = jnp.where(causal, 1.0, 0.0).astype(BF16)
    glane = lax.broadcasted_iota(jnp.int32, (1, 2 * n_heads), 1)

    def conv_silu(src_ref, w_ref, t0, first):
        if first:
            ext = jnp.concatenate([jnp.zeros((halo, src_ref.shape[2]), F32),
                                   src_ref[0, 0:L, :].astype(F32)], axis=0)
        else:
            start = pl.multiple_of(t0 - halo, halo)
            ext = src_ref[0, pl.ds(start, L + halo), :].astype(F32)
        y = jnp.zeros((L, ext.shape[1]), F32)
        for j in range(conv_width):
            off = halo - (conv_width - 1) + j
            y = y + ext[off:off + L, :] * w_ref[j:j + 1, :]
        return y * _sigmoid(y)

    def to_row(col):
        return jnp.sum(jnp.where(eye, col, 0.0), axis=0, keepdims=True)

    def chunk(ci, carry, first=False):
        t0 = 0 if first else pl.multiple_of(ci * L, L)
        g = gate_ref[0, pl.ds(t0, L), :]
        i_all = g + big_ref[...]
        lf_hi, lf_lo = _split_bf16(_log_sigmoid_neg(-(g + bfg_ref[...])))
        a_all = _dot(prefix, lf_hi) + _dot(prefix, lf_lo)
        qc = conv_silu(q_ref, cwq_ref, t0, first)
        kc = conv_silu(k_ref, cwk_ref, t0, first) * (dh ** -0.5)
        new_carry = []
        for hh in range(heads_per_step):
            ct, n_row, m_st = carry[hh]
            col = hp * heads_per_step + hh
            pick_i = glane == col
            pick_f = glane == (n_heads + col)
            i_col = jnp.sum(jnp.where(pick_i, i_all, 0.0), axis=1, keepdims=True)
            a_col = jnp.sum(jnp.where(pick_f, a_all, 0.0), axis=1, keepdims=True)
            a_row = to_row(a_col)
            i_row = to_row(i_col)
            sl = slice(hh * dh, (hh + 1) * dh)
            q_h = qc[:, sl].astype(BF16)
            k_h = kc[:, sl]
            k_hb = k_h.astype(BF16)
            v_h = v_ref[0, pl.ds(t0, L), sl]

            log_d = jnp.where(causal, a_col - a_row + i_row, NEG_INF)
            log_inter = a_col + m_st
            m_row = jnp.maximum(jnp.max(log_d, axis=-1, keepdims=True), log_inter)
            w_intra = jnp.exp(log_d - m_row)
            w_inter = jnp.exp(log_inter - m_row)
            s_qk = _dot(q_h, k_hb, NT_DIMS) * w_intra
            num = _dot(s_qk.astype(BF16), v_h) + w_inter * _dot(q_h, ct.astype(BF16))
            qn = jnp.sum(q_h.astype(F32) * n_row, axis=-1, keepdims=True)
            den = jnp.sum(s_qk, axis=-1, keepdims=True) + w_inter * qn
            h = num / jnp.maximum(jnp.abs(den), jnp.exp(-m_row))

            a_end = a_col[L - 1:L, :]
            log_w = a_end - a_col + i_col
            m_new = jnp.maximum(a_end + m_st, jnp.max(log_w, axis=0, keepdims=True))
            w_s = jnp.exp(log_w - m_new)
            decay = jnp.exp(a_end + m_st - m_new)
            kw = k_h * w_s
            ct_new = decay * ct + _dot(kw.astype(BF16), v_h, TN_DIMS)
            n_new = decay * n_row + jnp.sum(kw, axis=0, keepdims=True)
            new_carry.append((ct_new, n_new, m_new))

            ms = jnp.mean(h * h, axis=-1, keepdims=True)
            og = _sigmoid(o_ref[0, pl.ds(t0, L), sl].astype(F32))
            y = h * lax.rsqrt(ms + EPS) * gout_ref[0, hh:hh + 1, :] * og
            y_ref[0, pl.ds(t0, L), sl] = y.astype(y_ref.dtype)
        return tuple(new_carry)

    init = tuple((jnp.zeros((dh, dh), F32), jnp.zeros((1, dh), F32), jnp.zeros((1, 1), F32))
                 for _ in range(heads_per_step))
    lax.fori_loop(1, s // L, chunk, chunk(0, init, first=True))


def _mlstm(p, gates, conv_w, b_i, b_f, g_out, col0, n_heads, dh, heads_per_step):
    bsz, s, _ = p.shape
    w = heads_per_step * dh
    nstep = n_heads // heads_per_step
    width = n_heads * dh
    cb = col0 // w
    wb = width // w
    conv_width = conv_w.shape[0]
    bias = jnp.concatenate([b_i, b_f]).reshape(1, 2 * n_heads)
    lane = jnp.arange(2 * n_heads) < n_heads
    big = jnp.where(lane, bias, 0.0)
    bfg = jnp.where(lane, 0.0, bias)
    kern = functools.partial(_ml_kernel, heads_per_step=heads_per_step, dh=dh, conv_width=conv_width)
    seq = lambda off: pl.BlockSpec((1, s, w), lambda b, h: (b, 0, cb + off + h))
    return pl.pallas_call(
        kern,
        grid=(bsz, nstep),
        in_specs=[seq(0), seq(wb), seq(2 * wb), seq(3 * wb),
                  pl.BlockSpec((1, s, 2 * n_heads), lambda b, h: (b, 0, 0)),
                  pl.BlockSpec((conv_width, w), lambda b, h: (0, h)),
                  pl.BlockSpec((conv_width, w), lambda b, h: (0, wb + h)),
                  pl.BlockSpec((1, 2 * n_heads), lambda b, h: (0, 0)),
                  pl.BlockSpec((1, 2 * n_heads), lambda b, h: (0, 0)),
                  pl.BlockSpec((1, heads_per_step, dh), lambda b, h: (h, 0, 0))],
        out_specs=pl.BlockSpec((1, s, w), lambda b, h: (b, 0, h)),
        out_shape=jax.ShapeDtypeStruct((bsz, s, width), BF16),
        compiler_params=_params("parallel", "parallel"),
    )(p, p, p, p, gates, conv_w, conv_w, big, bfg,
      g_out.reshape(nstep, heads_per_step, dh))


def _fold_kernel(keys_ref, wq_ref, o_ref):
    o_ref[...] = _dot_f32(keys_ref[0], wq_ref[...], NT_DIMS).astype(BF16)


def _fold_keys(keys, w_q):
    nsk, nk, half = keys.shape
    d = w_q.shape[0]
    return pl.pallas_call(
        _fold_kernel,
        grid=(nsk,),
        in_specs=[pl.BlockSpec((1, nk, half), lambda i: (i, 0, 0)),
                  pl.BlockSpec((d, half), lambda i: (0, i))],
        out_specs=pl.BlockSpec((nk, d), lambda i: (i, 0)),
        out_shape=jax.ShapeDtypeStruct((nsk * nk, d), BF16),
        compiler_params=_params("parallel"),
    )(keys, w_q)


def _outproj_kernel(ysb_ref, yml_ref, x_ref, mod_ref, g_ref, wo_ref, wk_ref, x1_ref, h2_ref, sc_ref):
    wsb = ysb_ref.shape[2]
    mix = _dot(ysb_ref[0], wo_ref[0:wsb, :]) + _dot(yml_ref[0], wo_ref[wsb:, :])
    x1 = x_ref[0] + mod_ref[0, 2:3, :] * mix
    x1_ref[0] = x1
    h2 = _modulated_norm(x1, g_ref[...], mod_ref[0, 4:5, :], mod_ref[0, 3:4, :]).astype(BF16)
    h2_ref[0] = h2
    nk = sc_ref.shape[2]
    scores = _dot(wk_ref[...], h2, NT_DIMS)
    for i in range(sc_ref.shape[0]):
        sc_ref[i, 0] = scores[i * nk:(i + 1) * nk, :]


def _outproj(ysb, yml, x, mod, g, w_out, w_keys, nk, tm):
    bsz, s, d = x.shape
    nsk = w_keys.shape[0] // nk
    return pl.pallas_call(
        _outproj_kernel,
        grid=(bsz, s // tm),
        in_specs=[pl.BlockSpec((1, tm, ysb.shape[2]), lambda b, i: (b, i, 0)),
                  pl.BlockSpec((1, tm, yml.shape[2]), lambda b, i: (b, i, 0)),
                  pl.BlockSpec((1, tm, d), lambda b, i: (b, i, 0)),
                  pl.BlockSpec((1, N_MOD, d), lambda b, i: (b, 0, 0)),
                  pl.BlockSpec((1, d), lambda b, i: (0, 0)),
                  pl.BlockSpec(w_out.shape, lambda b, i: (0, 0)),
                  pl.BlockSpec(w_keys.shape, lambda b, i: (0, 0))],
        out_specs=[pl.BlockSpec((1, tm, d), lambda b, i: (b, i, 0)),
                   pl.BlockSpec((1, tm, d), lambda b, i: (b, i, 0)),
                   pl.BlockSpec((nsk, 1, nk, tm), lambda b, i: (0, b, 0, i))],
        out_shape=[jax.ShapeDtypeStruct((bsz, s, d), F32),
                   jax.ShapeDtypeStruct((bsz, s, d), BF16),
                   jax.ShapeDtypeStruct((nsk, bsz, nk, s), F32)],
        compiler_params=_params("parallel", "parallel"),
    )(ysb, yml, x, mod, g, w_out, w_keys)


def _extract_top(x, k, pos):
    n = x.shape[0]
    rank = jnp.full(x.shape, float(k), F32)
    tops = []
    for a in range(k):
        m = jnp.max(x, axis=0, keepdims=True)
        first = jnp.min(jnp.where(x == m, pos, float(n)), axis=0, keepdims=True)
        pick = pos == first
        rank = jnp.where(pick, float(a), rank)
        x = jnp.where(pick, NEG_INF, x)
        tops.append(m)
    return tops, rank


def _extract_top_fast(x, k):
    rank = jnp.full(x.shape, float(k), F32)
    tops = []
    for a in range(k):
        m = jnp.max(x, axis=0, keepdims=True)
        pick = x == m
        rank = jnp.where(pick, float(a), rank)
        x = jnp.where(pick, NEG_INF, x)
        tops.append(m)
    return tops, rank


SUBLANES = 8
COUNT_BITS = 5
COUNT_MASK = (1 << COUNT_BITS) - 1
ROUTE_ROWS = PEER_TOPK + SUBLANES * (PEER_TOPK // 2 - 1) + PEER_TOPK // 2


def _route_kernel(sc_ref, cc_ref, rank_ref, e1_ref, top_ref, cand_ref, sel_ref, rk_ref):
    n_heads = cc_ref.shape[0]
    nk, tb = sc_ref.shape[2], sc_ref.shape[3]
    K = PEER_TOPK
    assert K == 2 * SUBLANES
    pos = lax.broadcasted_iota(jnp.int32, (nk, tb), 0).astype(F32)
    cpos = lax.broadcasted_iota(jnp.int32, (ROUTE_ROWS, tb), 0).astype(F32)
    sub = lax.broadcasted_iota(jnp.int32, (SUBLANES, tb), 0)
    group = lambda a: slice(K + SUBLANES * (a - 1), K + SUBLANES * a)
    tail = slice(K + SUBLANES * (K // 2 - 1), ROUTE_ROWS)

    def tied(rank):
        n_ranked = jnp.sum(jnp.where(rank < float(K), 1.0, 0.0), axis=0, keepdims=True)
        return jnp.max(n_ranked) > float(K)

    def head(h, carry):
        s0 = sc_ref[2 * h, 0]
        s1 = sc_ref[2 * h + 1, 0]
        def keep(p, tops, rank):
            for a in range(K):
                top_ref[p, a:a + 1, :] = tops[a]
            rk_ref[p] = rank

        fast = [_extract_top_fast(x, K) for x in (s0, s1)]
        for p in range(2):
            keep(p, *fast[p])
        retry = [tied(fast[p][1]) for p in range(2)]
        for p, x in ((0, s0), (1, s1)):
            pl.when(retry[p])(lambda p=p, x=x: keep(p, *_extract_top(x, K, pos)))
        rank0, rank1 = rk_ref[0], rk_ref[1]
        top0, top1 = top_ref[0, 0:1, :], top_ref[1, 0:1, :]

        cand_ref[0:K, :] = top0 + top_ref[1]
        head1 = top_ref[1, 0:SUBLANES, :]
        for a in range(1, K // 2):
            cand_ref[group(a), :] = jnp.where(sub < K // (a + 1), top_ref[0, a:a + 1, :] + head1, NEG_INF)
        cand_ref[tail, :] = top_ref[0, K // 2:K, :] + top1
        cand = cand_ref[...]

        def keep_sel(pick_round):
            sel_ref[...] = jnp.where(pick_round < float(K), 1.0, 0.0)

        _, pick_round = _extract_top_fast(cand, K)
        keep_sel(pick_round)
        pl.when(tied(pick_round))(lambda: keep_sel(_extract_top(cand_ref[...], K, cpos)[1]))
        z = jnp.sum(sel_ref[...] * jnp.exp(cand - (top0 + top1)), axis=0, keepdims=True)

        cnt = jnp.where(rank0 == 0.0, jnp.sum(sel_ref[0:K, :], axis=0, keepdims=True), 0.0)
        for a in range(1, K // 2):
            cnt = jnp.where(rank0 == float(a), jnp.sum(sel_ref[group(a), :], axis=0, keepdims=True), cnt)
        for a in range(K // 2, K):
            row = tail.start + a - K // 2
            cnt = jnp.where(rank0 == float(a), sel_ref[row:row + 1, :], cnt)
        c0_bits = pltpu.bitcast(jnp.exp(s0 - top0) / z, jnp.int32)
        cc_ref[h, 0] = pltpu.bitcast((c0_bits & ~COUNT_MASK) | cnt.astype(jnp.int32), F32)
        rank_ref[h, 0] = pltpu.bitcast(rank1.astype(BF16), jnp.uint32)
        e1_ref[h, 0] = pltpu.bitcast(jnp.exp(s1 - top1).astype(BF16), jnp.uint32)
        return carry

    lax.fori_loop(0, n_heads, head, 0)


def _route(scores, tb):
    nsk, bsz, nk, s = scores.shape
    n_heads = nsk // 2
    spec = lambda rows: pl.BlockSpec((n_heads, 1, rows, tb), lambda b, i: (0, b, 0, i))
    words = jax.ShapeDtypeStruct((n_heads, bsz, nk // 2, s), jnp.uint32)
    return pl.pallas_call(
        _route_kernel,
        grid=(bsz, s // tb),
        in_specs=[pl.BlockSpec((nsk, 1, nk, tb), lambda b, i: (0, b, 0, i))],
        out_specs=[spec(nk), spec(nk // 2), spec(nk // 2)],
        out_shape=[jax.ShapeDtypeStruct((n_heads, bsz, nk, s), F32), words, words],
        scratch_shapes=[pltpu.VMEM((2, PEER_TOPK, tb), F32),
                        pltpu.VMEM((ROUTE_ROWS, tb), F32),
                        pltpu.VMEM((ROUTE_ROWS, tb), F32),
                        pltpu.VMEM((2, nk, tb), F32)],
        compiler_params=_params("parallel", "parallel"),
    )(scores)


def _gelu(x):
    return 0.5 * x * (1.0 + lax.erf(x * math.sqrt(0.5)))


def _peer_kernel(pace_ref, h2_ref, u_ref, vt_ref, cc_ref, rank_ref, e1_ref, o_ref, act_ref, coef_ref,
                 *, groups, slice_lanes):
    j = pl.program_id(2)
    n_heads, _, nk, tb = cc_ref.shape
    te = groups * nk
    rows = BF16_SUBLANES
    words = rows // 2
    n_slices = tb // slice_lanes
    n_sub = 2 * n_slices
    part = lambda k: (k // n_slices, slice((k % n_slices) * slice_lanes, (k % n_slices + 1) * slice_lanes))

    def activations(k):
        slot, tok = part(k)
        act_ref[slot, :, tok] = _dot(u_ref[slot * te:(slot + 1) * te, :], h2_ref[0, tok, :], NT_DIMS)

    def apply(k):
        slot, tok = part(k)
        o_ref[0, :, tok] += _dot(vt_ref[slot], coef_ref[slot, :, tok])

    def coefficients(k):
        slot, tok = part(k)
        assert 2 * groups == SUBLANES
        g0 = pl.multiple_of(j * (2 * groups), 2 * groups)
        scale = [cc_ref[h, 0, pl.ds(g0, 2 * groups), tok] for h in range(n_heads)]
        count = [(pltpu.bitcast(t, jnp.int32) & COUNT_MASK).astype(F32) for t in scale]
        for gg in range(groups):
            row = slot * groups + gg
            for l0 in range(tok.start, tok.stop, LANES):
                ln = slice(l0, l0 + LANES)
                lt = slice(l0 - tok.start, l0 - tok.start + LANES)
                cf = [None] * (nk // rows)
                for h in range(n_heads):
                    spread = lambda t: jnp.broadcast_to(t[row:row + 1, lt], (rows, LANES)).astype(BF16)
                    n0, c0 = spread(count[h]), spread(scale[h])
                    for rb in range(nk // rows):
                        wr = slice(rb * words, (rb + 1) * words)
                        rank = pltpu.bitcast(rank_ref[h, 0, wr, ln], BF16)
                        e1 = pltpu.bitcast(e1_ref[h, 0, wr, ln], BF16)
                        term = jnp.where(rank < n0, e1 * c0, 0.0)
                        cf[rb] = term if h == 0 else cf[rb] + term
                for rb in range(nk // rows):
                    er = slice(gg * nk + rb * rows, gg * nk + (rb + 1) * rows)
                    coef_ref[slot, er, ln] = cf[rb] * _gelu(act_ref[slot, er, ln]).astype(BF16)

    @pl.when(j == 0)
    def _():
        o_ref[...] = jnp.zeros_like(o_ref)

    paced = pace_ref[0] > 0
    pl.when(paced)(lambda: activations(0))
    for k in range(n_sub):
        @pl.when(paced)
        def _(k=k):
            if k + 1 < n_sub:
                activations(k + 1)
            if k > 0:
                apply(k - 1)
            coefficients(k)
    pl.when(paced)(lambda: apply(n_sub - 1))


def _peer(h2, u, vt3, cc, rank, e1, tb, groups):
    bsz, s, d = h2.shape
    n_heads, _, nk, _ = cc.shape
    te = groups * nk
    wspec = pl.BlockSpec((n_heads, 1, nk // 2, tb), lambda b, i, j: (0, b, 0, i))
    return pl.pallas_call(
        functools.partial(_peer_kernel, groups=groups, slice_lanes=min(2 * LANES, tb)),
        grid=(bsz, s // tb, u.shape[0] // (2 * te)),
        in_specs=[pl.BlockSpec(memory_space=pltpu.SMEM),
                  pl.BlockSpec((1, tb, d), lambda b, i, j: (b, i, 0)),
                  pl.BlockSpec((2 * te, d), lambda b, i, j: (j, 0)),
                  pl.BlockSpec((2, d, te), lambda b, i, j: (j, 0, 0)),
                  pl.BlockSpec((n_heads, 1, nk, tb), lambda b, i, j: (0, b, 0, i)),
                  wspec, wspec],
        out_specs=pl.BlockSpec((1, d, tb), lambda b, i, j: (b, 0, i)),
        out_shape=jax.ShapeDtypeStruct((bsz, d, s), F32),
        scratch_shapes=[pltpu.VMEM((2, te, tb), F32), pltpu.VMEM((2, te, tb), BF16)],
        compiler_params=_params("parallel", "parallel", "arbitrary"),
    )(jnp.ones((1,), jnp.int32), h2, u, vt3, cc, rank, e1)


def _residual_kernel(x1_ref, yt_ref, mod_ref, o_ref):
    o_ref[0] = x1_ref[0] + mod_ref[0, 5:6, :] * yt_ref[0].T


def _residual(x1, yt, mod, tm):
    bsz, s, d = x1.shape
    return pl.pallas_call(
        _residual_kernel,
        grid=(bsz, s // tm),
        in_specs=[pl.BlockSpec((1, tm, d), lambda b, i: (b, i, 0)),
                  pl.BlockSpec((1, d, tm), lambda b, i: (b, 0, i)),
                  pl.BlockSpec((1, N_MOD, d), lambda b, i: (b, 0, 0))],
        out_specs=pl.BlockSpec((1, tm, d), lambda b, i: (b, i, 0)),
        out_shape=jax.ShapeDtypeStruct((bsz, s, d), F32),
        compiler_params=_params("parallel", "parallel"),
    )(x1, yt, mod)


def _layer(x, c, w_ada, b_ada, g_norm1, w_in, b_igate, b_fgate, conv_w, g_q_sb, g_k_sb,
           g_out_sb, g_out_ml, w_out, g_norm2, w_q_peer, sub_keys, expert_u, expert_v):
    bsz, s, d = x.shape
    sb_heads, sb_hd = g_out_sb.shape
    ml_heads, ml_hd = g_out_ml.shape
    sb_width = sb_heads * sb_hd
    ml_width = ml_heads * ml_hd
    n_main = 3 * sb_width + 4 * ml_width
    peer_heads, _, n_keys, half = sub_keys.shape

    tm = min(512, s)
    tq = min(256, s)
    tb = min(1024, s)

    mod = _ada(c, w_ada, b_ada).reshape(bsz, N_MOD, d)
    p, gates = _inproj(x, mod, g_norm1.reshape(1, d), w_in[:, :n_main].astype(BF16),
                       w_in[:, n_main:].astype(BF16), tm)

    tile2 = lambda g: jnp.concatenate([g, g], axis=-1)
    ysb = _sb_attention(p, tile2(g_q_sb).reshape(1, 2 * sb_hd), tile2(g_k_sb).reshape(1, 2 * sb_hd),
                        g_out_sb.reshape(sb_heads // 2, 1, 2 * sb_hd), sb_heads, sb_hd, tq)
    yml = _mlstm(p, gates, conv_w, b_igate, b_fgate, g_out_ml, 3 * sb_width, ml_heads, ml_hd,
                 heads_per_step=ml_heads)

    w_keys = _fold_keys(sub_keys.reshape(2 * peer_heads, n_keys, half), w_q_peer)
    x1, h2, scores = _outproj(ysb, yml, x, mod, g_norm2.reshape(1, d), w_out.astype(BF16),
                              w_keys, n_keys, min(256, s))
    cc, rank, e1 = _route(scores, min(512, s))
    groups = 4
    te = groups * n_keys
    vt3 = expert_v.astype(BF16).reshape(-1, te, d).transpose(0, 2, 1)
    yt = _peer(h2, expert_u.astype(BF16), vt3, cc, rank, e1, tb, groups)
    return _residual(x1, yt, mod, tm)


def kernel(x, c, w_ada, b_ada, g_norm1, w_in, b_igate, b_fgate, conv_w, g_q_sb, g_k_sb, g_out_sb,
           g_out_ml, w_out, g_norm2, w_q_peer, sub_keys, expert_u, expert_v):
    params = (w_ada, b_ada, g_norm1, w_in, b_igate, b_fgate, conv_w, g_q_sb, g_k_sb, g_out_sb,
              g_out_ml, w_out, g_norm2, w_q_peer, sub_keys, expert_u, expert_v)
    for layer in range(w_ada.shape[0]):
        x = _layer(x, c, *(t[layer] for t in params))
    return x
```

```python
import functools
import math

import jax
import jax.numpy as jnp
from jax import lax
from jax.experimental import pallas as pl
from jax.experimental.pallas import tpu as pltpu

F32 = jnp.float32
BF16 = jnp.bfloat16

EPS = 1e-6
N_MOD = 6
MLSTM_CHUNK = 64
PEER_TOPK = 16
NEG_INF = float("-inf")
F32_EXP_UNDERFLOW = math.log(2.0 ** -126)

V7X_VMEM_BYTES = 64 * 1024 * 1024
VMEM_LIMIT = V7X_VMEM_BYTES - 12 * 1024 * 1024
LANES = 128
BF16_SUBLANES = 16

NT_DIMS = (((1,), (1,)), ((), ()))
NN_DIMS = (((1,), (0,)), ((), ()))
TN_DIMS = (((0,), (0,)), ((), ()))


def _params(*sem):
    return pltpu.CompilerParams(dimension_semantics=sem, vmem_limit_bytes=VMEM_LIMIT)


def _dot(a, b, dims=NN_DIMS):
    return lax.dot_general(a, b, dims, preferred_element_type=F32)


def _split_bf16(a):
    hi = a.astype(BF16)
    lo = (a - hi.astype(F32)).astype(BF16)
    return hi, lo


def _dot_f32(a, b, dims=NN_DIMS):
    ah, al = _split_bf16(a)
    bh, bl = _split_bf16(b)
    return _dot(ah, bh, dims) + (_dot(ah, bl, dims) + _dot(al, bh, dims))


def _dot_exact_rhs(a, b_bf16, dims=NN_DIMS):
    ah, al = _split_bf16(a)
    return _dot(ah, b_bf16, dims) + _dot(al, b_bf16, dims)


def _sigmoid(x):
    return 1.0 / (1.0 + jnp.exp(-x))


def _log_sigmoid_neg(z):
    return -(jnp.maximum(z, 0.0) + jnp.log(1.0 + jnp.exp(-jnp.abs(z))))


def _ada_kernel(c_ref, w_ref, b_ref, o_ref):
    c = c_ref[...]
    o_ref[...] = _dot_f32(c * _sigmoid(c), w_ref[...]) + b_ref[...]


def _ada(c, w, b):
    bsz, d = c.shape
    n = w.shape[1]
    tn = n // N_MOD
    return pl.pallas_call(
        _ada_kernel,
        grid=(n // tn,),
        in_specs=[pl.BlockSpec((bsz, d), lambda j: (0, 0)),
                  pl.BlockSpec((d, tn), lambda j: (0, j)),
                  pl.BlockSpec((1, tn), lambda j: (0, j))],
        out_specs=pl.BlockSpec((bsz, tn), lambda j: (0, j)),
        out_shape=jax.ShapeDtypeStruct((bsz, n), F32),
        compiler_params=_params("parallel"),
    )(c, w, b.reshape(1, n))


def _modulated_norm(x, g, scale, shift):
    y = x * lax.rsqrt(jnp.mean(x * x, axis=-1, keepdims=True) + EPS)
    return (y * g) * (1.0 + scale) + shift


def _inproj_kernel(x_ref, mod_ref, g_ref, w_ref, wg_ref, p_ref, gate_ref, *, n_chunk):
    h = _modulated_norm(x_ref[0], g_ref[...], mod_ref[0, 1:2, :], mod_ref[0, 0:1, :])
    hb = h.astype(BF16)
    for n0 in range(0, w_ref.shape[1], n_chunk):
        p_ref[0, :, n0:n0 + n_chunk] = _dot(hb, w_ref[:, n0:n0 + n_chunk]).astype(BF16)
    gate_ref[0] = _dot(hb, wg_ref[...])


def _inproj(x, mod, g, w_main, w_gate, tm):
    bsz, s, d = x.shape
    n = w_main.shape[1]
    ng = w_gate.shape[1]
    return pl.pallas_call(
        functools.partial(_inproj_kernel, n_chunk=512),
        grid=(bsz, s // tm),
        in_specs=[pl.BlockSpec((1, tm, d), lambda b, i: (b, i, 0)),
                  pl.BlockSpec((1, N_MOD, d), lambda b, i: (b, 0, 0)),
                  pl.BlockSpec((1, d), lambda b, i: (0, 0)),
                  pl.BlockSpec((d, n), lambda b, i: (0, 0)),
                  pl.BlockSpec((d, ng), lambda b, i: (0, 0))],
        out_specs=[pl.BlockSpec((1, tm, n), lambda b, i: (b, i, 0)),
                   pl.BlockSpec((1, tm, ng), lambda b, i: (b, i, 0))],
        out_shape=[jax.ShapeDtypeStruct((bsz, s, n), BF16),
                   jax.ShapeDtypeStruct((bsz, s, ng), F32)],
        compiler_params=_params("parallel", "parallel"),
    )(x, mod, g, w_main, w_gate)


def _pair_rms(x, g2, hd):
    lane = lax.broadcasted_iota(jnp.int32, x.shape, 1)
    sq = x * x
    s0 = jnp.sum(jnp.where(lane < hd, sq, 0.0), axis=-1, keepdims=True)
    s1 = jnp.sum(jnp.where(lane < hd, 0.0, sq), axis=-1, keepdims=True)
    inv = jnp.where(lane < hd, lax.rsqrt(s0 / hd + EPS), lax.rsqrt(s1 / hd + EPS))
    return x * inv * g2


def _sb_kernel(q_ref, k_ref, v_ref, gq_ref, gk_ref, go_ref, o_ref, kn_ref, acc_ref, run_ref,
               *, tq, hd):
    qi = pl.program_id(2)
    s = k_ref.shape[1]
    rows = 512 if s % 512 == 0 else tq

    @pl.when(qi == 0)
    def _():
        for r0 in range(0, s, rows):
            kk = k_ref[0, r0:r0 + rows, :].astype(F32)
            kn_ref[r0:r0 + rows, :] = _pair_rms(kk, gk_ref[...], hd).astype(BF16)

    qn = _pair_rms(q_ref[0].astype(F32), gq_ref[...], hd) * (hd ** -0.5)
    lane = lax.broadcasted_iota(jnp.int32, qn.shape, 1)
    q2 = jnp.concatenate([jnp.where(lane < hd, qn, 0.0), jnp.where(lane < hd, 0.0, qn)],
                         axis=0).astype(BF16)

    kw = min(2 * tq, s)
    r = lax.broadcasted_iota(jnp.int32, (kw, kw), 0)
    c = lax.broadcasted_iota(jnp.int32, (kw, kw), 1)
    suffix = jnp.where(r >= c, 1.0, 0.0).astype(BF16)
    first = jnp.maximum(qi - 1, 0) * tq if kw > tq else 0
    r2 = lax.broadcasted_iota(jnp.int32, (2 * tq, kw), 0)
    c2 = lax.broadcasted_iota(jnp.int32, (2 * tq, kw), 1)
    visible = first + c2 < qi * tq + jnp.where(r2 < tq, r2, r2 - tq)

    def sweep(k0, width, masked):
        if not isinstance(k0, int):
            k0 = pl.multiple_of(k0, tq)
        z = _dot(q2, kn_ref[pl.ds(k0, width), :], NT_DIMS)
        lk = _log_sigmoid_neg(z)
        if masked:
            lk = jnp.where(visible, lk, 0.0)
        cs = _dot_exact_rhs(lk, suffix[:width, :width])
        run = run_ref[...]
        w = jnp.exp(z + cs + run)
        if masked:
            w = jnp.where(visible, w, 0.0)
        acc_ref[...] += _dot(w.astype(BF16), v_ref[0, pl.ds(k0, width), :])
        run = run + cs[:, 0:1]
        run_ref[...] = run
        return jnp.max(run)

    acc_ref[...] = jnp.zeros_like(acc_ref)
    run_ref[...] = jnp.zeros_like(run_ref)
    z_bound = 1.02 * (hd ** 0.5) * jnp.max(jnp.abs(gq_ref[...])) * jnp.max(jnp.abs(gk_ref[...]))

    def more(state):
        j, top = state
        return jnp.logical_and(j >= 0, top + z_bound > F32_EXP_UNDERFLOW)

    lax.while_loop(more, lambda st: (st[0] - 1, sweep(st[0] * tq, tq, False)),
                   (qi - kw // tq, sweep(first, kw, True)))

    y = jnp.where(lane < hd, acc_ref[0:tq, :], acc_ref[tq:2 * tq, :])
    o_ref[0] = _pair_rms(y, go_ref[0], hd).astype(o_ref.dtype)


def _sb_attention(p, g_q2, g_k2, g_out2, n_heads, hd, tq):
    bsz, s, _ = p.shape
    w = 2 * hd
    npair = n_heads // 2
    return pl.pallas_call(
        functools.partial(_sb_kernel, tq=tq, hd=hd),
        grid=(bsz, npair, s // tq),
        in_specs=[pl.BlockSpec((1, tq, w), lambda b, h, i: (b, i, h)),
                  pl.BlockSpec((1, s, w), lambda b, h, i: (b, 0, npair + h)),
                  pl.BlockSpec((1, s, w), lambda b, h, i: (b, 0, 2 * npair + h)),
                  pl.BlockSpec((1, w), lambda b, h, i: (0, 0)),
                  pl.BlockSpec((1, w), lambda b, h, i: (0, 0)),
                  pl.BlockSpec((1, 1, w), lambda b, h, i: (h, 0, 0))],
        out_specs=pl.BlockSpec((1, tq, w), lambda b, h, i: (b, i, h)),
        out_shape=jax.ShapeDtypeStruct((bsz, s, n_heads * hd), BF16),
        scratch_shapes=[pltpu.VMEM((s, w), BF16), pltpu.VMEM((2 * tq, w), F32),
                        pltpu.VMEM((2 * tq, 1), F32)],
        compiler_params=_params("parallel", "parallel", "arbitrary"),
    )(p, p, p, g_q2, g_k2, g_out2)


def _ml_kernel(q_ref, k_ref, v_ref, o_ref, gate_ref, cwq_ref, cwk_ref, big_ref, bfg_ref,
               gout_ref, y_ref, *, heads_per_step, dh, conv_width):
    L = MLSTM_CHUNK
    s = q_ref.shape[1]
    n_heads = gate_ref.shape[2] // 2
    hp = pl.program_id(1)
    halo = BF16_SUBLANES

    r = lax.broadcasted_iota(jnp.int32, (L, L), 0)
    c = lax.broadcasted_iota(jnp.int32, (L, L), 1)
    eye = r == c
    causal = c <= r
    prefix = jnp.where(causal, 1.0, 0.0).astype(BF16)
    glane = lax.broadcasted_iota(jnp.int32, (1, 2 * n_heads), 1)

    def conv_silu(src_ref, w_ref, t0, first):
        if first:
            ext = jnp.concatenate([jnp.zeros((halo, src_ref.shape[2]), F32),
                                   src_ref[0, 0:L, :].astype(F32)], axis=0)
        else:
            start = pl.multiple_of(t0 - halo, halo)
            ext = src_ref[0, pl.ds(start, L + halo), :].astype(F32)
        y = jnp.zeros((L, ext.shape[1]), F32)
        for j in range(conv_width):
            off = halo - (conv_width - 1) + j
            y = y + ext[off:off + L, :] * w_ref[j:j + 1, :]
        return y * _sigmoid(y)

    def to_row(col):
        return jnp.sum(jnp.where(eye, col, 0.0), axis=0, keepdims=True)

    def chunk(ci, carry, first=False):
        t0 = 0 if first else pl.multiple_of(ci * L, L)
        g = gate_ref[0, pl.ds(t0, L), :]
        i_all = g + big_ref[...]
        lf_hi, lf_lo = _split_bf16(_log_sigmoid_neg(-(g + bfg_ref[...])))
        a_all = _dot(prefix, lf_hi) + _dot(prefix, lf_lo)
        qc = conv_silu(q_ref, cwq_ref, t0, first)
        kc = conv_silu(k_ref, cwk_ref, t0, first) * (dh ** -0.5)
        new_carry = []
        for hh in range(heads_per_step):
            ct, n_row, m_st = carry[hh]
            col = hp * heads_per_step + hh
            pick_i = glane == col
            pick_f = glane == (n_heads + col)
            i_col = jnp.sum(jnp.where(pick_i, i_all, 0.0), axis=1, keepdims=True)
            a_col = jnp.sum(jnp.where(pick_f, a_all, 0.0), axis=1, keepdims=True)
            a_row = to_row(a_col)
            i_row = to_row(i_col)
            sl = slice(hh * dh, (hh + 1) * dh)
            q_h = qc[:, sl].astype(BF16)
            k_h = kc[:, sl]
            k_hb = k_h.astype(BF16)
            v_h = v_ref[0, pl.ds(t0, L), sl]

            log_d = jnp.where(causal, a_col - a_row + i_row, NEG_INF)
            log_inter = a_col + m_st
            m_row = jnp.maximum(jnp.max(log_d, axis=-1, keepdims=True), log_inter)
            w_intra = jnp.exp(log_d - m_row)
            w_inter = jnp.exp(log_inter - m_row)
            s_qk = _dot(q_h, k_hb, NT_DIMS) * w_intra
            num = _dot(s_qk.astype(BF16), v_h) + w_inter * _dot(q_h, ct.astype(BF16))
            qn = jnp.sum(q_h.astype(F32) * n_row, axis=-1, keepdims=True)
            den = jnp.sum(s_qk, axis=-1, keepdims=True) + w_inter * qn
            h = num / jnp.maximum(jnp.abs(den), jnp.exp(-m_row))

            a_end = a_col[L - 1:L, :]
            log_w = a_end - a_col + i_col
            m_new = jnp.maximum(a_end + m_st, jnp.max(log_w, axis=0, keepdims=True))
            w_s = jnp.exp(log_w - m_new)
            decay = jnp.exp(a_end + m_st - m_new)
            kw = k_h * w_s
            ct_new = decay * ct + _dot(kw.astype(BF16), v_h, TN_DIMS)
            n_new = decay * n_row + jnp.sum(kw, axis=0, keepdims=True)
            new_carry.append((ct_new, n_new, m_new))

            ms = jnp.mean(h * h, axis=-1, keepdims=True)
            og = _sigmoid(o_ref[0, pl.ds(t0, L), sl].astype(F32))
            y = h * lax.rsqrt(ms + EPS) * gout_ref[0, hh:hh + 1, :] * og
            y_ref[0, pl.ds(t0, L), sl] = y.astype(y_ref.dtype)
        return tuple(new_carry)

    init = tuple((jnp.zeros((dh, dh), F32), jnp.zeros((1, dh), F32), jnp.zeros((1, 1), F32))
                 for _ in range(heads_per_step))
    lax.fori_loop(1, s // L, chunk, chunk(0, init, first=True))


def _mlstm(p, gates, conv_w, b_i, b_f, g_out, col0, n_heads, dh, heads_per_step):
    bsz, s, _ = p.shape
    w = heads_per_step * dh
    nstep = n_heads // heads_per_step
    width = n_heads * dh
    cb = col0 // w
    wb = width // w
    conv_width = conv_w.shape[0]
    bias = jnp.concatenate([b_i, b_f]).reshape(1, 2 * n_heads)
    lane = jnp.arange(2 * n_heads) < n_heads
    big = jnp.where(lane, bias, 0.0)
    bfg = jnp.where(lane, 0.0, bias)
    kern = functools.partial(_ml_kernel, heads_per_step=heads_per_step, dh=dh, conv_width=conv_width)
    seq = lambda off: pl.BlockSpec((1, s, w), lambda b, h: (b, 0, cb + off + h))
    return pl.pallas_call(
        kern,
        grid=(bsz, nstep),
        in_specs=[seq(0), seq(wb), seq(2 * wb), seq(3 * wb),
                  pl.BlockSpec((1, s, 2 * n_heads), lambda b, h: (b, 0, 0)),
                  pl.BlockSpec((conv_width, w), lambda b, h: (0, h)),
                  pl.BlockSpec((conv_width, w), lambda b, h: (0, wb + h)),
                  pl.BlockSpec((1, 2 * n_heads), lambda b, h: (0, 0)),
                  pl.BlockSpec((1, 2 * n_heads), lambda b, h: (0, 0)),
                  pl.BlockSpec((1, heads_per_step, dh), lambda b, h: (h, 0, 0))],
        out_specs=pl.BlockSpec((1, s, w), lambda b, h: (b, 0, h)),
        out_shape=jax.ShapeDtypeStruct((bsz, s, width), BF16),
        compiler_params=_params("parallel", "parallel"),
    )(p, p, p, p, gates, conv_w, conv_w, big, bfg,
      g_out.reshape(nstep, heads_per_step, dh))


def _fold_kernel(keys_ref, wq_ref, o_ref):
    o_ref[...] = _dot_f32(keys_ref[0], wq_ref[...], NT_DIMS).astype(BF16)


def _fold_keys(keys, w_q):
    nsk, nk, half = keys.shape
    d = w_q.shape[0]
    return pl.pallas_call(
        _fold_kernel,
        grid=(nsk,),
        in_specs=[pl.BlockSpec((1, nk, half), lambda i: (i, 0, 0)),
                  pl.BlockSpec((d, half), lambda i: (0, i))],
        out_specs=pl.BlockSpec((nk, d), lambda i: (i, 0)),
        out_shape=jax.ShapeDtypeStruct((nsk * nk, d), BF16),
        compiler_params=_params("parallel"),
    )(keys, w_q)


def _outproj_kernel(ysb_ref, yml_ref, x_ref, mod_ref, g_ref, wo_ref, wk_ref, x1_ref, h2_ref, sc_ref):
    wsb = ysb_ref.shape[2]
    mix = _dot(ysb_ref[0], wo_ref[0:wsb, :]) + _dot(yml_ref[0], wo_ref[wsb:, :])
    x1 = x_ref[0] + mod_ref[0, 2:3, :] * mix
    x1_ref[0] = x1
    h2 = _modulated_norm(x1, g_ref[...], mod_ref[0, 4:5, :], mod_ref[0, 3:4, :]).astype(BF16)
    h2_ref[0] = h2
    nk = sc_ref.shape[2]
    scores = _dot(wk_ref[...], h2, NT_DIMS)
    for i in range(sc_ref.shape[0]):
        sc_ref[i, 0] = scores[i * nk:(i + 1) * nk, :]


def _outproj(ysb, yml, x, mod, g, w_out, w_keys, nk, tm):
    bsz, s, d = x.shape
    nsk = w_keys.shape[0] // nk
    return pl.pallas_call(
        _outproj_kernel,
        grid=(bsz, s // tm),
        in_specs=[pl.BlockSpec((1, tm, ysb.shape[2]), lambda b, i: (b, i, 0)),
                  pl.BlockSpec((1, tm, yml.shape[2]), lambda b, i: (b, i, 0)),
                  pl.BlockSpec((1, tm, d), lambda b, i: (b, i, 0)),
                  pl.BlockSpec((1, N_MOD, d), lambda b, i: (b, 0, 0)),
                  pl.BlockSpec((1, d), lambda b, i: (0, 0)),
                  pl.BlockSpec(w_out.shape, lambda b, i: (0, 0)),
                  pl.BlockSpec(w_keys.shape, lambda b, i: (0, 0))],
        out_specs=[pl.BlockSpec((1, tm, d), lambda b, i: (b, i, 0)),
                   pl.BlockSpec((1, tm, d), lambda b, i: (b, i, 0)),
                   pl.BlockSpec((nsk, 1, nk, tm), lambda b, i: (0, b, 0, i))],
        out_shape=[jax.ShapeDtypeStruct((bsz, s, d), F32),
                   jax.ShapeDtypeStruct((bsz, s, d), BF16),
                   jax.ShapeDtypeStruct((nsk, bsz, nk, s), F32)],
        compiler_params=_params("parallel", "parallel"),
    )(ysb, yml, x, mod, g, w_out, w_keys)


def _extract_top(x, k, pos):
    n = x.shape[0]
    rank = jnp.full(x.shape, float(k), F32)
    tops = []
    for a in range(k):
        m = jnp.max(x, axis=0, keepdims=True)
        first = jnp.min(jnp.where(x == m, pos, float(n)), axis=0, keepdims=True)
        pick = pos == first
        rank = jnp.where(pick, float(a), rank)
        x = jnp.where(pick, NEG_INF, x)
        tops.append(m)
    return tops, rank


def _extract_top_fast(x, k):
    rank = jnp.full(x.shape, float(k), F32)
    tops = []
    for a in range(k):
        m = jnp.max(x, axis=0, keepdims=True)
        pick = x == m
        rank = jnp.where(pick, float(a), rank)
        x = jnp.where(pick, NEG_INF, x)
        tops.append(m)
    return tops, rank


SUBLANES = 8
COUNT_BITS = 5
COUNT_MASK = (1 << COUNT_BITS) - 1
ROUTE_ROWS = PEER_TOPK + SUBLANES * (PEER_TOPK // 2 - 1) + PEER_TOPK // 2


def _route_kernel(sc_ref, cc_ref, rank_ref, e1_ref, top_ref, cand_ref, sel_ref, rk_ref):
    n_heads = cc_ref.shape[0]
    nk, tb = sc_ref.shape[2], sc_ref.shape[3]
    K = PEER_TOPK
    assert K == 2 * SUBLANES
    pos = lax.broadcasted_iota(jnp.int32, (nk, tb), 0).astype(F32)
    cpos = lax.broadcasted_iota(jnp.int32, (ROUTE_ROWS, tb), 0).astype(F32)
    sub = lax.broadcasted_iota(jnp.int32, (SUBLANES, tb), 0)
    group = lambda a: slice(K + SUBLANES * (a - 1), K + SUBLANES * a)
    tail = slice(K + SUBLANES * (K // 2 - 1), ROUTE_ROWS)

    def tied(rank):
        n_ranked = jnp.sum(jnp.where(rank < float(K), 1.0, 0.0), axis=0, keepdims=True)
        return jnp.max(n_ranked) > float(K)

    def head(h, carry):
        s0 = sc_ref[2 * h, 0]
        s1 = sc_ref[2 * h + 1, 0]
        def keep(p, tops, rank):
            for a in range(K):
                top_ref[p, a:a + 1, :] = tops[a]
            rk_ref[p] = rank

        fast = [_extract_top_fast(x, K) for x in (s0, s1)]
        for p in range(2):
            keep(p, *fast[p])
        retry = [tied(fast[p][1]) for p in range(2)]
        for p, x in ((0, s0), (1, s1)):
            pl.when(retry[p])(lambda p=p, x=x: keep(p, *_extract_top(x, K, pos)))
        rank0, rank1 = rk_ref[0], rk_ref[1]
        top0, top1 = top_ref[0, 0:1, :], top_ref[1, 0:1, :]

        cand_ref[0:K, :] = top0 + top_ref[1]
        head1 = top_ref[1, 0:SUBLANES, :]
        for a in range(1, K // 2):
            cand_ref[group(a), :] = jnp.where(sub < K // (a + 1), top_ref[0, a:a + 1, :] + head1, NEG_INF)
        cand_ref[tail, :] = top_ref[0, K // 2:K, :] + top1
        cand = cand_ref[...]

        def keep_sel(pick_round):
            sel_ref[...] = jnp.where(pick_round < float(K), 1.0, 0.0)

        _, pick_round = _extract_top_fast(cand, K)
        keep_sel(pick_round)
        pl.when(tied(pick_round))(lambda: keep_sel(_extract_top(cand_ref[...], K, cpos)[1]))
        z = jnp.sum(sel_ref[...] * jnp.exp(cand - (top0 + top1)), axis=0, keepdims=True)

        cnt = jnp.where(rank0 == 0.0, jnp.sum(sel_ref[0:K, :], axis=0, keepdims=True), 0.0)
        for a in range(1, K // 2):
            cnt = jnp.where(rank0 == float(a), jnp.sum(sel_ref[group(a), :], axis=0, keepdims=True), cnt)
        for a in range(K // 2, K):
            row = tail.start + a - K // 2
            cnt = jnp.where(rank0 == float(a), sel_ref[row:row + 1, :], cnt)
        c0_bits = pltpu.bitcast(jnp.exp(s0 - top0) * (GELU_SCALE / z), jnp.int32)
        cc_ref[h, 0] = pltpu.bitcast((c0_bits & ~COUNT_MASK) | cnt.astype(jnp.int32), F32)
        rank_ref[h, 0] = pltpu.bitcast(rank1.astype(BF16), jnp.uint32)
        e1_ref[h, 0] = pltpu.bitcast(jnp.exp(s1 - top1).astype(BF16), jnp.uint32)
        return carry

    lax.fori_loop(0, n_heads, head, 0)


def _route(scores, tb):
    nsk, bsz, nk, s = scores.shape
    n_heads = nsk // 2
    spec = lambda rows: pl.BlockSpec((n_heads, 1, rows, tb), lambda b, i: (0, b, 0, i))
    words = jax.ShapeDtypeStruct((n_heads, bsz, nk // 2, s), jnp.uint32)
    return pl.pallas_call(
        _route_kernel,
        grid=(bsz, s // tb),
        in_specs=[pl.BlockSpec((nsk, 1, nk, tb), lambda b, i: (0, b, 0, i))],
        out_specs=[spec(nk), spec(nk // 2), spec(nk // 2)],
        out_shape=[jax.ShapeDtypeStruct((n_heads, bsz, nk, s), F32), words, words],
        scratch_shapes=[pltpu.VMEM((2, PEER_TOPK, tb), F32),
                        pltpu.VMEM((ROUTE_ROWS, tb), F32),
                        pltpu.VMEM((ROUTE_ROWS, tb), F32),
                        pltpu.VMEM((2, nk, tb), F32)],
        compiler_params=_params("parallel", "parallel"),
    )(scores)


GELU_SCALE = math.sqrt(0.5)


def _gelu_core(y):
    return y * (1.0 + lax.erf(y))


def _peer_kernel(pace_ref, h2_ref, u_ref, vt_ref, cc_ref, rank_ref, e1_ref, o_ref, act_ref, coef_ref,
                 *, groups, slice_lanes):
    j = pl.program_id(2)
    n_heads, _, nk, tb = cc_ref.shape
    te = groups * nk
    rows = BF16_SUBLANES
    words = rows // 2
    n_slices = tb // slice_lanes
    n_sub = 2 * n_slices
    part = lambda k: (k // n_slices, slice((k % n_slices) * slice_lanes, (k % n_slices + 1) * slice_lanes))

    def activations(k):
        slot, tok = part(k)
        act_ref[slot, :, tok] = _dot(u_ref[slot * te:(slot + 1) * te, :], h2_ref[0, tok, :], NT_DIMS)

    def apply(k):
        slot, tok = part(k)
        o_ref[0, :, tok] += _dot(vt_ref[slot], coef_ref[slot, :, tok])

    def coefficients(k):
        slot, tok = part(k)
        assert 2 * groups == SUBLANES
        g0 = pl.multiple_of(j * (2 * groups), 2 * groups)
        scale = [cc_ref[h, 0, pl.ds(g0, 2 * groups), tok] for h in range(n_heads)]
        count = [(pltpu.bitcast(t, jnp.int32) & COUNT_MASK).astype(F32) for t in scale]
        for gg in range(groups):
            row = slot * groups + gg
            for l0 in range(tok.start, tok.stop, LANES):
                ln = slice(l0, l0 + LANES)
                lt = slice(l0 - tok.start, l0 - tok.start + LANES)
                cf = [None] * (nk // rows)
                for h in range(n_heads):
                    spread = lambda t: jnp.broadcast_to(t[row:row + 1, lt], (rows, LANES)).astype(BF16)
                    n0, c0 = spread(count[h]), spread(scale[h])
                    for rb in range(nk // rows):
                        wr = slice(rb * words, (rb + 1) * words)
                        rank = pltpu.bitcast(rank_ref[h, 0, wr, ln], BF16)
                        e1 = pltpu.bitcast(e1_ref[h, 0, wr, ln], BF16)
                        term = jnp.where(rank < n0, e1 * c0, 0.0)
                        cf[rb] = term if h == 0 else cf[rb] + term
                for rb in range(nk // rows):
                    er = slice(gg * nk + rb * rows, gg * nk + (rb + 1) * rows)
                    coef_ref[slot, er, ln] = cf[rb] * _gelu_core(act_ref[slot, er, ln]).astype(BF16)

    @pl.when(j == 0)
    def _():
        o_ref[...] = jnp.zeros_like(o_ref)

    paced = pace_ref[0] > 0
    pl.when(paced)(lambda: activations(0))
    for k in range(n_sub):
        @pl.when(paced)
        def _(k=k):
            if k + 1 < n_sub:
                activations(k + 1)
            if k > 0:
                apply(k - 1)
            coefficients(k)
    pl.when(paced)(lambda: apply(n_sub - 1))


def _peer(h2, u, vt3, cc, rank, e1, tb, groups):
    bsz, s, d = h2.shape
    n_heads, _, nk, _ = cc.shape
    te = groups * nk
    wspec = pl.BlockSpec((n_heads, 1, nk // 2, tb), lambda b, i, j: (0, b, 0, i))
    return pl.pallas_call(
        functools.partial(_peer_kernel, groups=groups, slice_lanes=min(2 * LANES, tb)),
        grid=(bsz, s // tb, u.shape[0] // (2 * te)),
        in_specs=[pl.BlockSpec(memory_space=pltpu.SMEM),
                  pl.BlockSpec((1, tb, d), lambda b, i, j: (b, i, 0)),
                  pl.BlockSpec((2 * te, d), lambda b, i, j: (j, 0)),
                  pl.BlockSpec((2, d, te), lambda b, i, j: (j, 0, 0)),
                  pl.BlockSpec((n_heads, 1, nk, tb), lambda b, i, j: (0, b, 0, i)),
                  wspec, wspec],
        out_specs=pl.BlockSpec((1, d, tb), lambda b, i, j: (b, 0, i)),
        out_shape=jax.ShapeDtypeStruct((bsz, d, s), F32),
        scratch_shapes=[pltpu.VMEM((2, te, tb), F32), pltpu.VMEM((2, te, tb), BF16)],
        compiler_params=_params("parallel", "parallel", "arbitrary"),
    )(jnp.ones((1,), jnp.int32), h2, u, vt3, cc, rank, e1)


def _residual_kernel(x1_ref, yt_ref, mod_ref, o_ref):
    o_ref[0] = x1_ref[0] + mod_ref[0, 5:6, :] * yt_ref[0].T


def _residual(x1, yt, mod, tm):
    bsz, s, d = x1.shape
    return pl.pallas_call(
        _residual_kernel,
        grid=(bsz, s // tm),
        in_specs=[pl.BlockSpec((1, tm, d), lambda b, i: (b, i, 0)),
                  pl.BlockSpec((1, d, tm), lambda b, i: (b, 0, i)),
                  pl.BlockSpec((1, N_MOD, d), lambda b, i: (b, 0, 0))],
        out_specs=pl.BlockSpec((1, tm, d), lambda b, i: (b, i, 0)),
        out_shape=jax.ShapeDtypeStruct((bsz, s, d), F32),
        compiler_params=_params("parallel", "parallel"),
    )(x1, yt, mod)


def _layer(x, c, w_ada, b_ada, g_norm1, w_in, b_igate, b_fgate, conv_w, g_q_sb, g_k_sb,
           g_out_sb, g_out_ml, w_out, g_norm2, w_q_peer, sub_keys, expert_u, expert_v):
    bsz, s, d = x.shape
    sb_heads, sb_hd = g_out_sb.shape
    ml_heads, ml_hd = g_out_ml.shape
    sb_width = sb_heads * sb_hd
    ml_width = ml_heads * ml_hd
    n_main = 3 * sb_width + 4 * ml_width
    peer_heads, _, n_keys, half = sub_keys.shape

    tm = min(512, s)
    tq = min(256, s)
    tb = min(1024, s)

    mod = _ada(c, w_ada, b_ada).reshape(bsz, N_MOD, d)
    p, gates = _inproj(x, mod, g_norm1.reshape(1, d), w_in[:, :n_main].astype(BF16),
                       w_in[:, n_main:].astype(BF16), tm)

    tile2 = lambda g: jnp.concatenate([g, g], axis=-1)
    ysb = _sb_attention(p, tile2(g_q_sb).reshape(1, 2 * sb_hd), tile2(g_k_sb).reshape(1, 2 * sb_hd),
                        g_out_sb.reshape(sb_heads // 2, 1, 2 * sb_hd), sb_heads, sb_hd, tq)
    yml = _mlstm(p, gates, conv_w, b_igate, b_fgate, g_out_ml, 3 * sb_width, ml_heads, ml_hd,
                 heads_per_step=ml_heads)

    w_keys = _fold_keys(sub_keys.reshape(2 * peer_heads, n_keys, half), w_q_peer)
    x1, h2, scores = _outproj(ysb, yml, x, mod, g_norm2.reshape(1, d), w_out.astype(BF16),
                              w_keys, n_keys, min(256, s))
    cc, rank, e1 = _route(scores, min(512, s))
    groups = 4
    te = groups * n_keys
    vt3 = expert_v.astype(BF16).reshape(-1, te, d).transpose(0, 2, 1)
    yt = _peer(h2, (expert_u * GELU_SCALE).astype(BF16), vt3, cc, rank, e1, tb, groups)
    return _residual(x1, yt, mod, tm)


def kernel(x, c, w_ada, b_ada, g_norm1, w_in, b_igate, b_fgate, conv_w, g_q_sb, g_k_sb, g_out_sb,
           g_out_ml, w_out, g_norm2, w_q_peer, sub_keys, expert_u, expert_v):
    params = (w_ada, b_ada, g_norm1, w_in, b_igate, b_fgate, conv_w, g_q_sb, g_k_sb, g_out_sb,
              g_out_ml, w_out, g_norm2, w_q_peer, sub_keys, expert_u, expert_v)
    for layer in range(w_ada.shape[0]):
        x = _layer(x, c, *(t[layer] for t in params))
    return x
```

```python
import functools
import math

import jax
import jax.numpy as jnp
from jax import lax
from jax.experimental import pallas as pl
from jax.experimental.pallas import tpu as pltpu

F32 = jnp.float32
BF16 = jnp.bfloat16

EPS = 1e-6
N_MOD = 6
MLSTM_CHUNK = 64
PEER_TOPK = 16
NEG_INF = float("-inf")
F32_EXP_UNDERFLOW = math.log(2.0 ** -126)

V7X_VMEM_BYTES = 64 * 1024 * 1024
VMEM_LIMIT = V7X_VMEM_BYTES - 12 * 1024 * 1024
LANES = 128
BF16_SUBLANES = 16

NT_DIMS = (((1,), (1,)), ((), ()))
NN_DIMS = (((1,), (0,)), ((), ()))
TN_DIMS = (((0,), (0,)), ((), ()))


def _params(*sem):
    return pltpu.CompilerParams(dimension_semantics=sem, vmem_limit_bytes=VMEM_LIMIT)


def _dot(a, b, dims=NN_DIMS):
    return lax.dot_general(a, b, dims, preferred_element_type=F32)


def _split_bf16(a):
    hi = a.astype(BF16)
    lo = (a - hi.astype(F32)).astype(BF16)
    return hi, lo


def _dot_f32(a, b, dims=NN_DIMS):
    ah, al = _split_bf16(a)
    bh, bl = _split_bf16(b)
    return _dot(ah, bh, dims) + (_dot(ah, bl, dims) + _dot(al, bh, dims))


def _dot_exact_rhs(a, b_bf16, dims=NN_DIMS):
    ah, al = _split_bf16(a)
    return _dot(ah, b_bf16, dims) + _dot(al, b_bf16, dims)


def _sigmoid(x):
    return 1.0 / (1.0 + jnp.exp(-x))


def _log_sigmoid_neg(z):
    return -(jnp.maximum(z, 0.0) + jnp.log(1.0 + jnp.exp(-jnp.abs(z))))


def _ada_kernel(c_ref, w_ref, b_ref, o_ref):
    c = c_ref[...]
    o_ref[...] = _dot_f32(c * _sigmoid(c), w_ref[...]) + b_ref[...]


def _ada(c, w, b):
    bsz, d = c.shape
    n = w.shape[1]
    tn = n // N_MOD
    return pl.pallas_call(
        _ada_kernel,
        grid=(n // tn,),
        in_specs=[pl.BlockSpec((bsz, d), lambda j: (0, 0)),
                  pl.BlockSpec((d, tn), lambda j: (0, j)),
                  pl.BlockSpec((1, tn), lambda j: (0, j))],
        out_specs=pl.BlockSpec((bsz, tn), lambda j: (0, j)),
        out_shape=jax.ShapeDtypeStruct((bsz, n), F32),
        compiler_params=_params("parallel"),
    )(c, w, b.reshape(1, n))


def _modulated_norm(x, g, scale, shift):
    y = x * lax.rsqrt(jnp.mean(x * x, axis=-1, keepdims=True) + EPS)
    return (y * g) * (1.0 + scale) + shift


def _inproj_kernel(x_ref, mod_ref, g_ref, w_ref, wg_ref, p_ref, gate_ref, *, n_chunk):
    h = _modulated_norm(x_ref[0], g_ref[...], mod_ref[0, 1:2, :], mod_ref[0, 0:1, :])
    hb = h.astype(BF16)
    for n0 in range(0, w_ref.shape[1], n_chunk):
        p_ref[0, :, n0:n0 + n_chunk] = _dot(hb, w_ref[:, n0:n0 + n_chunk]).astype(BF16)
    gate_ref[0] = _dot(hb, wg_ref[...])


def _inproj(x, mod, g, w_main, w_gate, tm):
    bsz, s, d = x.shape
    n = w_main.shape[1]
    ng = w_gate.shape[1]
    return pl.pallas_call(
        functools.partial(_inproj_kernel, n_chunk=512),
        grid=(bsz, s // tm),
        in_specs=[pl.BlockSpec((1, tm, d), lambda b, i: (b, i, 0)),
                  pl.BlockSpec((1, N_MOD, d), lambda b, i: (b, 0, 0)),
                  pl.BlockSpec((1, d), lambda b, i: (0, 0)),
                  pl.BlockSpec((d, n), lambda b, i: (0, 0)),
                  pl.BlockSpec((d, ng), lambda b, i: (0, 0))],
        out_specs=[pl.BlockSpec((1, tm, n), lambda b, i: (b, i, 0)),
                   pl.BlockSpec((1, tm, ng), lambda b, i: (b, i, 0))],
        out_shape=[jax.ShapeDtypeStruct((bsz, s, n), BF16),
                   jax.ShapeDtypeStruct((bsz, s, ng), F32)],
        compiler_params=_params("parallel", "parallel"),
    )(x, mod, g, w_main, w_gate)


def _pair_rms(x, g2, hd):
    lane = lax.broadcasted_iota(jnp.int32, x.shape, 1)
    sq = x * x
    s0 = jnp.sum(jnp.where(lane < hd, sq, 0.0), axis=-1, keepdims=True)
    s1 = jnp.sum(jnp.where(lane < hd, 0.0, sq), axis=-1, keepdims=True)
    inv = jnp.where(lane < hd, lax.rsqrt(s0 / hd + EPS), lax.rsqrt(s1 / hd + EPS))
    return x * inv * g2


def _sb_kernel(q_ref, k_ref, v_ref, gq_ref, gk_ref, go_ref, o_ref, kn_ref, acc_ref, run_ref,
               *, tq, hd):
    qi = pl.program_id(2)
    s = k_ref.shape[1]
    rows = 512 if s % 512 == 0 else tq

    @pl.when(qi == 0)
    def _():
        for r0 in range(0, s, rows):
            kk = k_ref[0, r0:r0 + rows, :].astype(F32)
            kn_ref[r0:r0 + rows, :] = _pair_rms(kk, gk_ref[...], hd).astype(BF16)

    qn = _pair_rms(q_ref[0].astype(F32), gq_ref[...], hd) * (hd ** -0.5)
    lane = lax.broadcasted_iota(jnp.int32, qn.shape, 1)
    q2 = jnp.concatenate([jnp.where(lane < hd, qn, 0.0), jnp.where(lane < hd, 0.0, qn)],
                         axis=0).astype(BF16)

    kw = min(2 * tq, s)
    r = lax.broadcasted_iota(jnp.int32, (kw, kw), 0)
    c = lax.broadcasted_iota(jnp.int32, (kw, kw), 1)
    suffix = jnp.where(r >= c, 1.0, 0.0).astype(BF16)
    first = jnp.maximum(qi - 1, 0) * tq if kw > tq else 0
    r2 = lax.broadcasted_iota(jnp.int32, (2 * tq, kw), 0)
    c2 = lax.broadcasted_iota(jnp.int32, (2 * tq, kw), 1)
    visible = first + c2 < qi * tq + jnp.where(r2 < tq, r2, r2 - tq)

    def sweep(k0, width, masked):
        if not isinstance(k0, int):
            k0 = pl.multiple_of(k0, tq)
        z = _dot(q2, kn_ref[pl.ds(k0, width), :], NT_DIMS)
        lk = _log_sigmoid_neg(z)
        if masked:
            lk = jnp.where(visible, lk, 0.0)
        cs = _dot_exact_rhs(lk, suffix[:width, :width])
        run = run_ref[...]
        w = jnp.exp(z + cs + run)
        if masked:
            w = jnp.where(visible, w, 0.0)
        acc_ref[...] += _dot(w.astype(BF16), v_ref[0, pl.ds(k0, width), :])
        run = run + cs[:, 0:1]
        run_ref[...] = run
        return jnp.max(run)

    acc_ref[...] = jnp.zeros_like(acc_ref)
    run_ref[...] = jnp.zeros_like(run_ref)
    z_bound = 1.02 * (hd ** 0.5) * jnp.max(jnp.abs(gq_ref[...])) * jnp.max(jnp.abs(gk_ref[...]))

    def more(state):
        j, top = state
        return jnp.logical_and(j >= 0, top + z_bound > F32_EXP_UNDERFLOW)

    lax.while_loop(more, lambda st: (st[0] - 1, sweep(st[0] * tq, tq, False)),
                   (qi - kw // tq, sweep(first, kw, True)))

    y = jnp.where(lane < hd, acc_ref[0:tq, :], acc_ref[tq:2 * tq, :])
    o_ref[0] = _pair_rms(y, go_ref[0], hd).astype(o_ref.dtype)


def _sb_attention(p, g_q2, g_k2, g_out2, n_heads, hd, tq):
    bsz, s, _ = p.shape
    w = 2 * hd
    npair = n_heads // 2
    return pl.pallas_call(
        functools.partial(_sb_kernel, tq=tq, hd=hd),
        grid=(bsz, npair, s // tq),
        in_specs=[pl.BlockSpec((1, tq, w), lambda b, h, i: (b, i, h)),
                  pl.BlockSpec((1, s, w), lambda b, h, i: (b, 0, npair + h)),
                  pl.BlockSpec((1, s, w), lambda b, h, i: (b, 0, 2 * npair + h)),
                  pl.BlockSpec((1, w), lambda b, h, i: (0, 0)),
                  pl.BlockSpec((1, w), lambda b, h, i: (0, 0)),
                  pl.BlockSpec((1, 1, w), lambda b, h, i: (h, 0, 0))],
        out_specs=pl.BlockSpec((1, tq, w), lambda b, h, i: (b, i, h)),
        out_shape=jax.ShapeDtypeStruct((bsz, s, n_heads * hd), BF16),
        scratch_shapes=[pltpu.VMEM((s, w), BF16), pltpu.VMEM((2 * tq, w), F32),
                        pltpu.VMEM((2 * tq, 1), F32)],
        compiler_params=_params("parallel", "parallel", "arbitrary"),
    )(p, p, p, g_q2, g_k2, g_out2)


def _ml_kernel(q_ref, k_ref, v_ref, o_ref, gate_ref, cwq_ref, cwk_ref, big_ref, bfg_ref,
               gout_ref, y_ref, *, heads_per_step, dh, conv_width):
    L = MLSTM_CHUNK
    s = q_ref.shape[1]
    n_heads = gate_ref.shape[2] // 2
    hp = pl.program_id(1)
    halo = BF16_SUBLANES

    r = lax.broadcasted_iota(jnp.int32, (L, L), 0)
    c = lax.broadcasted_iota(jnp.int32, (L, L), 1)
    eye = r == c
    causal = c <= r
    prefix = jnp.where(causal, 1.0, 0.0).astype(BF16)
    glane = lax.broadcasted_iota(jnp.int32, (1, 2 * n_heads), 1)

    def conv_silu(src_ref, w_ref, t0, first):
        if first:
            ext = jnp.concatenate([jnp.zeros((halo, src_ref.shape[2]), F32),
                                   src_ref[0, 0:L, :].astype(F32)], axis=0)
        else:
            start = pl.multiple_of(t0 - halo, halo)
            ext = src_ref[0, pl.ds(start, L + halo), :].astype(F32)
        y = jnp.zeros((L, ext.shape[1]), F32)
        for j in range(conv_width):
            off = halo - (conv_width - 1) + j
            y = y + ext[off:off + L, :] * w_ref[j:j + 1, :]
        return y * _sigmoid(y)

    def to_row(col):
        return jnp.sum(jnp.where(eye, col, 0.0), axis=0, keepdims=True)

    def chunk(ci, carry, first=False):
        t0 = 0 if first else pl.multiple_of(ci * L, L)
        g = gate_ref[0, pl.ds(t0, L), :]
        i_all = g + big_ref[...]
        lf_hi, lf_lo = _split_bf16(_log_sigmoid_neg(-(g + bfg_ref[...])))
        a_all = _dot(prefix, lf_hi) + _dot(prefix, lf_lo)
        qc = conv_silu(q_ref, cwq_ref, t0, first)
        kc = conv_silu(k_ref, cwk_ref, t0, first) * (dh ** -0.5)
        new_carry = []
        for hh in range(heads_per_step):
            ct, n_row, m_st = carry[hh]
            col = hp * heads_per_step + hh
            pick_i = glane == col
            pick_f = glane == (n_heads + col)
            i_col = jnp.sum(jnp.where(pick_i, i_all, 0.0), axis=1, keepdims=True)
            a_col = jnp.sum(jnp.where(pick_f, a_all, 0.0), axis=1, keepdims=True)
            a_row = to_row(a_col)
            i_row = to_row(i_col)
            sl = slice(hh * dh, (hh + 1) * dh)
            q_h = qc[:, sl].astype(BF16)
            k_h = kc[:, sl]
            k_hb = k_h.astype(BF16)
            v_h = v_ref[0, pl.ds(t0, L), sl]

            log_d = jnp.where(causal, a_col - a_row + i_row, NEG_INF)
            log_inter = a_col + m_st
            m_row = jnp.maximum(jnp.max(log_d, axis=-1, keepdims=True), log_inter)
            w_intra = jnp.exp(log_d - m_row)
            w_inter = jnp.exp(log_inter - m_row)
            s_qk = _dot(q_h, k_hb, NT_DIMS) * w_intra
            num = _dot(s_qk.astype(BF16), v_h) + w_inter * _dot(q_h, ct.astype(BF16))
            qn = jnp.sum(q_h.astype(F32) * n_row, axis=-1, keepdims=True)
            den = jnp.sum(s_qk, axis=-1, keepdims=True) + w_inter * qn
            h = num / jnp.maximum(jnp.abs(den), jnp.exp(-m_row))

            a_end = a_col[L - 1:L, :]
            log_w = a_end - a_col + i_col
            m_new = jnp.maximum(a_end + m_st, jnp.max(log_w, axis=0, keepdims=True))
            w_s = jnp.exp(log_w - m_new)
            decay = jnp.exp(a_end + m_st - m_new)
            kw = k_h * w_s
            ct_new = decay * ct + _dot(kw.astype(BF16), v_h, TN_DIMS)
            n_new = decay * n_row + jnp.sum(kw, axis=0, keepdims=True)
            new_carry.append((ct_new, n_new, m_new))

            ms = jnp.mean(h * h, axis=-1, keepdims=True)
            og = _sigmoid(o_ref[0, pl.ds(t0, L), sl].astype(F32))
            y = h * lax.rsqrt(ms + EPS) * gout_ref[0, hh:hh + 1, :] * og
            y_ref[0, pl.ds(t0, L), sl] = y.astype(y_ref.dtype)
        return tuple(new_carry)

    init = tuple((jnp.zeros((dh, dh), F32), jnp.zeros((1, dh), F32), jnp.zeros((1, 1), F32))
                 for _ in range(heads_per_step))
    lax.fori_loop(1, s // L, chunk, chunk(0, init, first=True), unroll=2)


def _mlstm(p, gates, conv_w, b_i, b_f, g_out, col0, n_heads, dh, heads_per_step):
    bsz, s, _ = p.shape
    w = heads_per_step * dh
    nstep = n_heads // heads_per_step
    width = n_heads * dh
    cb = col0 // w
    wb = width // w
    conv_width = conv_w.shape[0]
    bias = jnp.concatenate([b_i, b_f]).reshape(1, 2 * n_heads)
    lane = jnp.arange(2 * n_heads) < n_heads
    big = jnp.where(lane, bias, 0.0)
    bfg = jnp.where(lane, 0.0, bias)
    kern = functools.partial(_ml_kernel, heads_per_step=heads_per_step, dh=dh, conv_width=conv_width)
    seq = lambda off: pl.BlockSpec((1, s, w), lambda b, h: (b, 0, cb + off + h))
    return pl.pallas_call(
        kern,
        grid=(bsz, nstep),
        in_specs=[seq(0), seq(wb), seq(2 * wb), seq(3 * wb),
                  pl.BlockSpec((1, s, 2 * n_heads), lambda b, h: (b, 0, 0)),
                  pl.BlockSpec((conv_width, w), lambda b, h: (0, h)),
                  pl.BlockSpec((conv_width, w), lambda b, h: (0, wb + h)),
                  pl.BlockSpec((1, 2 * n_heads), lambda b, h: (0, 0)),
                  pl.BlockSpec((1, 2 * n_heads), lambda b, h: (0, 0)),
                  pl.BlockSpec((1, heads_per_step, dh), lambda b, h: (h, 0, 0))],
        out_specs=pl.BlockSpec((1, s, w), lambda b, h: (b, 0, h)),
        out_shape=jax.ShapeDtypeStruct((bsz, s, width), BF16),
        compiler_params=_params("parallel", "parallel"),
    )(p, p, p, p, gates, conv_w, conv_w, big, bfg,
      g_out.reshape(nstep, heads_per_step, dh))


def _fold_kernel(keys_ref, wq_ref, o_ref):
    o_ref[...] = _dot_f32(keys_ref[0], wq_ref[...], NT_DIMS).astype(BF16)


def _fold_keys(keys, w_q):
    nsk, nk, half = keys.shape
    d = w_q.shape[0]
    return pl.pallas_call(
        _fold_kernel,
        grid=(nsk,),
        in_specs=[pl.BlockSpec((1, nk, half), lambda i: (i, 0, 0)),
                  pl.BlockSpec((d, half), lambda i: (0, i))],
        out_specs=pl.BlockSpec((nk, d), lambda i: (i, 0)),
        out_shape=jax.ShapeDtypeStruct((nsk * nk, d), BF16),
        compiler_params=_params("parallel"),
    )(keys, w_q)


def _outproj_kernel(ysb_ref, yml_ref, x_ref, mod_ref, g_ref, wo_ref, wk_ref, x1_ref, h2_ref, sc_ref):
    wsb = ysb_ref.shape[2]
    mix = _dot(ysb_ref[0], wo_ref[0:wsb, :]) + _dot(yml_ref[0], wo_ref[wsb:, :])
    x1 = x_ref[0] + mod_ref[0, 2:3, :] * mix
    x1_ref[0] = x1
    h2 = _modulated_norm(x1, g_ref[...], mod_ref[0, 4:5, :], mod_ref[0, 3:4, :]).astype(BF16)
    h2_ref[0] = h2
    nk = sc_ref.shape[2]
    scores = _dot(wk_ref[...], h2, NT_DIMS)
    for i in range(sc_ref.shape[0]):
        sc_ref[i, 0] = scores[i * nk:(i + 1) * nk, :]


def _outproj(ysb, yml, x, mod, g, w_out, w_keys, nk, tm):
    bsz, s, d = x.shape
    nsk = w_keys.shape[0] // nk
    return pl.pallas_call(
        _outproj_kernel,
        grid=(bsz, s // tm),
        in_specs=[pl.BlockSpec((1, tm, ysb.shape[2]), lambda b, i: (b, i, 0)),
                  pl.BlockSpec((1, tm, yml.shape[2]), lambda b, i: (b, i, 0)),
                  pl.BlockSpec((1, tm, d), lambda b, i: (b, i, 0)),
                  pl.BlockSpec((1, N_MOD, d), lambda b, i: (b, 0, 0)),
                  pl.BlockSpec((1, d), lambda b, i: (0, 0)),
                  pl.BlockSpec(w_out.shape, lambda b, i: (0, 0)),
                  pl.BlockSpec(w_keys.shape, lambda b, i: (0, 0))],
        out_specs=[pl.BlockSpec((1, tm, d), lambda b, i: (b, i, 0)),
                   pl.BlockSpec((1, tm, d), lambda b, i: (b, i, 0)),
                   pl.BlockSpec((nsk, 1, nk, tm), lambda b, i: (0, b, 0, i))],
        out_shape=[jax.ShapeDtypeStruct((bsz, s, d), F32),
                   jax.ShapeDtypeStruct((bsz, s, d), BF16),
                   jax.ShapeDtypeStruct((nsk, bsz, nk, s), F32)],
        compiler_params=_params("parallel", "parallel"),
    )(ysb, yml, x, mod, g, w_out, w_keys)


def _extract_top(x, k, pos):
    n = x.shape[0]
    rank = jnp.full(x.shape, float(k), F32)
    tops = []
    for a in range(k):
        m = jnp.max(x, axis=0, keepdims=True)
        first = jnp.min(jnp.where(x == m, pos, float(n)), axis=0, keepdims=True)
        pick = pos == first
        rank = jnp.where(pick, float(a), rank)
        x = jnp.where(pick, NEG_INF, x)
        tops.append(m)
    return tops, rank


def _extract_top_fast(x, k):
    rank = jnp.full(x.shape, float(k), F32)
    tops = []
    for a in range(k):
        m = jnp.max(x, axis=0, keepdims=True)
        pick = x == m
        rank = jnp.where(pick, float(a), rank)
        x = jnp.where(pick, NEG_INF, x)
        tops.append(m)
    return tops, rank


SUBLANES = 8
COUNT_BITS = 5
COUNT_MASK = (1 << COUNT_BITS) - 1
ROUTE_ROWS = PEER_TOPK + SUBLANES * (PEER_TOPK // 2 - 1) + PEER_TOPK // 2


def _route_kernel(sc_ref, cc_ref, rank_ref, e1_ref, top_ref, cand_ref, sel_ref, rk_ref):
    n_heads = cc_ref.shape[0]
    nk, tb = sc_ref.shape[2], sc_ref.shape[3]
    K = PEER_TOPK
    assert K == 2 * SUBLANES
    pos = lax.broadcasted_iota(jnp.int32, (nk, tb), 0).astype(F32)
    cpos = lax.broadcasted_iota(jnp.int32, (ROUTE_ROWS, tb), 0).astype(F32)
    sub = lax.broadcasted_iota(jnp.int32, (SUBLANES, tb), 0)
    group = lambda a: slice(K + SUBLANES * (a - 1), K + SUBLANES * a)
    tail = slice(K + SUBLANES * (K // 2 - 1), ROUTE_ROWS)

    def tied(rank):
        n_ranked = jnp.sum(jnp.where(rank < float(K), 1.0, 0.0), axis=0, keepdims=True)
        return jnp.max(n_ranked) > float(K)

    def head(h, carry):
        s0 = sc_ref[2 * h, 0]
        s1 = sc_ref[2 * h + 1, 0]
        def keep(p, tops, rank):
            for a in range(K):
                top_ref[p, a:a + 1, :] = tops[a]
            rk_ref[p] = rank

        fast = [_extract_top_fast(x, K) for x in (s0, s1)]
        for p in range(2):
            keep(p, *fast[p])
        retry = [tied(fast[p][1]) for p in range(2)]
        for p, x in ((0, s0), (1, s1)):
            pl.when(retry[p])(lambda p=p, x=x: keep(p, *_extract_top(x, K, pos)))
        rank0, rank1 = rk_ref[0], rk_ref[1]
        top0, top1 = top_ref[0, 0:1, :], top_ref[1, 0:1, :]

        cand_ref[0:K, :] = top0 + top_ref[1]
        head1 = top_ref[1, 0:SUBLANES, :]
        for a in range(1, K // 2):
            cand_ref[group(a), :] = jnp.where(sub < K // (a + 1), top_ref[0, a:a + 1, :] + head1, NEG_INF)
        cand_ref[tail, :] = top_ref[0, K // 2:K, :] + top1
        cand = cand_ref[...]

        def keep_sel(pick_round):
            sel_ref[...] = jnp.where(pick_round < float(K), 1.0, 0.0)

        _, pick_round = _extract_top_fast(cand, K)
        keep_sel(pick_round)
        pl.when(tied(pick_round))(lambda: keep_sel(_extract_top(cand_ref[...], K, cpos)[1]))
        z = jnp.sum(sel_ref[...] * jnp.exp(cand - (top0 + top1)), axis=0, keepdims=True)

        cnt = jnp.where(rank0 == 0.0, jnp.sum(sel_ref[0:K, :], axis=0, keepdims=True), 0.0)
        for a in range(1, K // 2):
            cnt = jnp.where(rank0 == float(a), jnp.sum(sel_ref[group(a), :], axis=0, keepdims=True), cnt)
        for a in range(K // 2, K):
            row = tail.start + a - K // 2
            cnt = jnp.where(rank0 == float(a), sel_ref[row:row + 1, :], cnt)
        c0_bits = pltpu.bitcast(jnp.exp(s0 - top0) * (GELU_SCALE / z), jnp.int32)
        cc_ref[h, 0] = pltpu.bitcast((c0_bits & ~COUNT_MASK) | cnt.astype(jnp.int32), F32)
        rank_ref[h, 0] = pltpu.bitcast(rank1.astype(BF16), jnp.uint32)
        e1_ref[h, 0] = pltpu.bitcast(jnp.exp(s1 - top1).astype(BF16), jnp.uint32)
        return carry

    lax.fori_loop(0, n_heads, head, 0)


def _route(scores, tb):
    nsk, bsz, nk, s = scores.shape
    n_heads = nsk // 2
    spec = lambda rows: pl.BlockSpec((n_heads, 1, rows, tb), lambda b, i: (0, b, 0, i))
    words = jax.ShapeDtypeStruct((n_heads, bsz, nk // 2, s), jnp.uint32)
    return pl.pallas_call(
        _route_kernel,
        grid=(bsz, s // tb),
        in_specs=[pl.BlockSpec((nsk, 1, nk, tb), lambda b, i: (0, b, 0, i))],
        out_specs=[spec(nk), spec(nk // 2), spec(nk // 2)],
        out_shape=[jax.ShapeDtypeStruct((n_heads, bsz, nk, s), F32), words, words],
        scratch_shapes=[pltpu.VMEM((2, PEER_TOPK, tb), F32),
                        pltpu.VMEM((ROUTE_ROWS, tb), F32),
                        pltpu.VMEM((ROUTE_ROWS, tb), F32),
                        pltpu.VMEM((2, nk, tb), F32)],
        compiler_params=_params("parallel", "parallel"),
    )(scores)


GELU_SCALE = math.sqrt(0.5)


def _gelu_core(y):
    return y * (1.0 + lax.erf(y))


def _peer_kernel(pace_ref, h2_ref, u_ref, vt_ref, cc_ref, rank_ref, e1_ref, o_ref, act_ref, coef_ref,
                 *, groups, slice_lanes):
    j = pl.program_id(2)
    n_heads, _, nk, tb = cc_ref.shape
    te = groups * nk
    rows = BF16_SUBLANES
    words = rows // 2
    n_slices = tb // slice_lanes
    n_sub = 2 * n_slices
    part = lambda k: (k // n_slices, slice((k % n_slices) * slice_lanes, (k % n_slices + 1) * slice_lanes))

    def activations(k):
        slot, tok = part(k)
        act_ref[slot, :, tok] = _dot(u_ref[slot * te:(slot + 1) * te, :], h2_ref[0, tok, :], NT_DIMS)

    def apply(k):
        slot, tok = part(k)
        o_ref[0, :, tok] += _dot(vt_ref[slot], coef_ref[slot, :, tok])

    def coefficients(k):
        slot, tok = part(k)
        assert 2 * groups == SUBLANES
        g0 = pl.multiple_of(j * (2 * groups), 2 * groups)
        scale = [cc_ref[h, 0, pl.ds(g0, 2 * groups), tok] for h in range(n_heads)]
        count = [(pltpu.bitcast(t, jnp.int32) & COUNT_MASK).astype(F32) for t in scale]
        for gg in range(groups):
            row = slot * groups + gg
            for l0 in range(tok.start, tok.stop, LANES):
                ln = slice(l0, l0 + LANES)
                lt = slice(l0 - tok.start, l0 - tok.start + LANES)
                cf = [None] * (nk // rows)
                for h in range(n_heads):
                    spread = lambda t: jnp.broadcast_to(t[row:row + 1, lt], (rows, LANES)).astype(BF16)
                    n0, c0 = spread(count[h]), spread(scale[h])
                    for rb in range(nk // rows):
                        wr = slice(rb * words, (rb + 1) * words)
                        rank = pltpu.bitcast(rank_ref[h, 0, wr, ln], BF16)
                        e1 = pltpu.bitcast(e1_ref[h, 0, wr, ln], BF16)
                        term = jnp.where(rank < n0, e1 * c0, 0.0)
                        cf[rb] = term if h == 0 else cf[rb] + term
                for rb in range(nk // rows):
                    er = slice(gg * nk + rb * rows, gg * nk + (rb + 1) * rows)
                    coef_ref[slot, er, ln] = cf[rb] * _gelu_core(act_ref[slot, er, ln]).astype(BF16)

    @pl.when(j == 0)
    def _():
        o_ref[...] = jnp.zeros_like(o_ref)

    paced = pace_ref[0] > 0
    pl.when(paced)(lambda: activations(0))
    for k in range(n_sub):
        @pl.when(paced)
        def _(k=k):
            if k + 1 < n_sub:
                activations(k + 1)
            if k > 0:
                apply(k - 1)
            coefficients(k)
    pl.when(paced)(lambda: apply(n_sub - 1))


def _peer(h2, u, vt3, cc, rank, e1, tb, groups):
    bsz, s, d = h2.shape
    n_heads, _, nk, _ = cc.shape
    te = groups * nk
    wspec = pl.BlockSpec((n_heads, 1, nk // 2, tb), lambda b, i, j: (0, b, 0, i))
    return pl.pallas_call(
        functools.partial(_peer_kernel, groups=groups, slice_lanes=min(2 * LANES, tb)),
        grid=(bsz, s // tb, u.shape[0] // (2 * te)),
        in_specs=[pl.BlockSpec(memory_space=pltpu.SMEM),
                  pl.BlockSpec((1, tb, d), lambda b, i, j: (b, i, 0)),
                  pl.BlockSpec((2 * te, d), lambda b, i, j: (j, 0)),
                  pl.BlockSpec((2, d, te), lambda b, i, j: (j, 0, 0)),
                  pl.BlockSpec((n_heads, 1, nk, tb), lambda b, i, j: (0, b, 0, i)),
                  wspec, wspec],
        out_specs=pl.BlockSpec((1, d, tb), lambda b, i, j: (b, 0, i)),
        out_shape=jax.ShapeDtypeStruct((bsz, d, s), F32),
        scratch_shapes=[pltpu.VMEM((2, te, tb), F32), pltpu.VMEM((2, te, tb), BF16)],
        compiler_params=_params("parallel", "parallel", "arbitrary"),
    )(jnp.ones((1,), jnp.int32), h2, u, vt3, cc, rank, e1)


def _residual_kernel(x1_ref, yt_ref, mod_ref, o_ref):
    o_ref[0] = x1_ref[0] + mod_ref[0, 5:6, :] * yt_ref[0].T


def _residual(x1, yt, mod, tm):
    bsz, s, d = x1.shape
    return pl.pallas_call(
        _residual_kernel,
        grid=(bsz, s // tm),
        in_specs=[pl.BlockSpec((1, tm, d), lambda b, i: (b, i, 0)),
                  pl.BlockSpec((1, d, tm), lambda b, i: (b, 0, i)),
                  pl.BlockSpec((1, N_MOD, d), lambda b, i: (b, 0, 0))],
        out_specs=pl.BlockSpec((1, tm, d), lambda b, i: (b, i, 0)),
        out_shape=jax.ShapeDtypeStruct((bsz, s, d), F32),
        compiler_params=_params("parallel", "parallel"),
    )(x1, yt, mod)


def _layer(x, c, w_ada, b_ada, g_norm1, w_in, b_igate, b_fgate, conv_w, g_q_sb, g_k_sb,
           g_out_sb, g_out_ml, w_out, g_norm2, w_q_peer, sub_keys, expert_u, expert_v):
    bsz, s, d = x.shape
    sb_heads, sb_hd = g_out_sb.shape
    ml_heads, ml_hd = g_out_ml.shape
    sb_width = sb_heads * sb_hd
    ml_width = ml_heads * ml_hd
    n_main = 3 * sb_width + 4 * ml_width
    peer_heads, _, n_keys, half = sub_keys.shape

    tm = min(512, s)
    tq = min(256, s)
    tb = min(1024, s)

    mod = _ada(c, w_ada, b_ada).reshape(bsz, N_MOD, d)
    p, gates = _inproj(x, mod, g_norm1.reshape(1, d), w_in[:, :n_main].astype(BF16),
                       w_in[:, n_main:].astype(BF16), tm)

    tile2 = lambda g: jnp.concatenate([g, g], axis=-1)
    ysb = _sb_attention(p, tile2(g_q_sb).reshape(1, 2 * sb_hd), tile2(g_k_sb).reshape(1, 2 * sb_hd),
                        g_out_sb.reshape(sb_heads // 2, 1, 2 * sb_hd), sb_heads, sb_hd, tq)
    yml = _mlstm(p, gates, conv_w, b_igate, b_fgate, g_out_ml, 3 * sb_width, ml_heads, ml_hd,
                 heads_per_step=ml_heads)

    w_keys = _fold_keys(sub_keys.reshape(2 * peer_heads, n_keys, half), w_q_peer)
    x1, h2, scores = _outproj(ysb, yml, x, mod, g_norm2.reshape(1, d), w_out.astype(BF16),
                              w_keys, n_keys, min(256, s))
    cc, rank, e1 = _route(scores, min(512, s))
    groups = 4
    te = groups * n_keys
    vt3 = expert_v.astype(BF16).reshape(-1, te, d).transpose(0, 2, 1)
    yt = _peer(h2, (expert_u * GELU_SCALE).astype(BF16), vt3, cc, rank, e1, tb, groups)
    return _residual(x1, yt, mod, tm)


def kernel(x, c, w_ada, b_ada, g_norm1, w_in, b_igate, b_fgate, conv_w, g_q_sb, g_k_sb, g_out_sb,
           g_out_ml, w_out, g_norm2, w_q_peer, sub_keys, expert_u, expert_v):
    params = (w_ada, b_ada, g_norm1, w_in, b_igate, b_fgate, conv_w, g_q_sb, g_k_sb, g_out_sb,
              g_out_ml, w_out, g_norm2, w_q_peer, sub_keys, expert_u, expert_v)
    for layer in range(w_ada.shape[0]):
        x = _layer(x, c, *(t[layer] for t in params))
    return x
```
